```python
import jax
import jax.numpy as jnp
from jax import lax

D_MODEL = 1024
BATCH = 16
SEQ = 2048
DEPTH = 4

N_MIXERS = 4
N_CONV_LAYERS = (DEPTH + 3) // 4
N_HGRN_LAYERS = (DEPTH + 2) // 4
N_MLSTM_LAYERS = (DEPTH + 1) // 4
N_SB_LAYERS = DEPTH // 4
D_FF = 2816
FFN_CONV = 3
CONV_WIDTH = 31
HGRN_EXPAND = 128
HGRN_HEADS = D_MODEL // HGRN_EXPAND
HGRN_FDIM = HGRN_HEADS * HGRN_EXPAND
HGRN_VDIM = D_MODEL // HGRN_HEADS
HGRN_CHUNK = 32
MLSTM_INNER = 2 * D_MODEL
MLSTM_HEADS = 4
MLSTM_HDIM = MLSTM_INNER // MLSTM_HEADS
MLSTM_CONV = 4
MLSTM_QKV_BLOCK = 4
MLSTM_CHUNK = 64
SB_HEADS = 16
SB_HDIM = D_MODEL // SB_HEADS
SB_BLOCK = 128
DN_ALPHA = (2.0 * DEPTH) ** 0.25
DN_BETA = (8.0 * DEPTH) ** -0.25
LN_EPS = 1e-5
RMS_EPS = 1e-6

kernel_name = 'hybrid_interleaved_conditioned_trunk'


def layer_norm(x, g, b):
    xf = x.astype(jnp.float32)
    mu = jnp.mean(xf, axis=-1, keepdims=True)
    var = jnp.mean(jnp.square(xf - mu), axis=-1, keepdims=True)
    y = (xf - mu) * lax.rsqrt(var + LN_EPS)
    return (y * g.astype(jnp.float32) + b.astype(jnp.float32)).astype(x.dtype)


def causal_dwconv(x, w, b):
    width = w.shape[0]
    xp = jnp.pad(x, ((0, 0), (width - 1, 0), (0, 0)))
    y = lax.conv_general_dilated(xp, w[:, None, :].astype(x.dtype), window_strides=(1,), padding='VALID',
                                 dimension_numbers=('NWC', 'WIO', 'NWC'), feature_group_count=x.shape[-1])
    return y + b


def adaln(c, w, b):
    m = jax.nn.silu(c) @ w + b
    return [t[:, None, :] for t in jnp.split(m, 6, axis=-1)]


def conformer_conv_module(h, w_pw1, b_pw1, dw_w, dw_b, ln_g, ln_b, w_pw2, b_pw2):
    u = jax.nn.glu(h @ w_pw1 + b_pw1, axis=-1)
    u = causal_dwconv(u, dw_w, dw_b)
    u = jax.nn.silu(layer_norm(u, ln_g, ln_b))
    return u @ w_pw2 + b_pw2


def hgrn2_chunked(q, k, v, log_f):
    B_, S_, H, dk = q.shape
    dv = v.shape[-1]
    C = HGRN_CHUNK
    N = S_ // C

    def to_chunks(t):
        return jnp.moveaxis(t.reshape(B_, N, C, H, t.shape[-1]), 1, 0)

    q, k, v, log_f = map(to_chunks, (q, k, v, log_f))
    b = jnp.cumsum(log_f, axis=2)
    b_ref = b[:, :, C // 2 - 1:C // 2]
    scores = jnp.einsum('nbthd,nbshd->nbhts', q * jnp.exp(b - b_ref), k * jnp.exp(b_ref - b))
    causal = jnp.tril(jnp.ones((C, C), dtype=bool))
    o_intra = jnp.einsum('nbhts,nbshv->nbthv', jnp.where(causal, scores, 0.0), v)
    b_last = b[:, :, -1]
    q_dec = q * jnp.exp(b)
    k_dec = k * jnp.exp(b_last[:, :, None] - b)

    def step(S, inp):
        qd, kd, vc, dec = inp
        o = jnp.einsum('bthd,bhdv->bthv', qd, S)
        S = dec[..., None] * S + jnp.einsum('bshd,bshv->bhdv', kd, vc)
        return S, o

    S0 = jnp.zeros((B_, H, dk, dv), jnp.float32)
    _, o_inter = lax.scan(step, S0, (q_dec, k_dec, v, jnp.exp(b_last)))
    o = o_intra + o_inter
    return jnp.moveaxis(o, 0, 1).reshape(B_, S_, H, dv)


def hgrn2_layer(h, lower_bound, w_in, norm_g, w_out):
    B_, S_, _ = h.shape
    proj = h @ w_in
    q_pre, f_pre, i_val, g_out = jnp.split(proj, [HGRN_FDIM, 2 * HGRN_FDIM, 2 * HGRN_FDIM + D_MODEL], axis=-1)
    lb = lower_bound.astype(jnp.float32)
    q = jax.nn.silu(q_pre.astype(jnp.float32))
    f = lb + (1.0 - lb) * jax.nn.sigmoid(f_pre.astype(jnp.float32))
    k = 1.0 - f
    heads = lambda t: t.reshape(B_, S_, HGRN_HEADS, -1)
    o = hgrn2_chunked(heads(q), heads(k), heads(i_val.astype(jnp.float32)), heads(jnp.log(f)))
    o = o * lax.rsqrt(jnp.mean(jnp.square(o), axis=-1, keepdims=True) + RMS_EPS) * norm_g.astype(jnp.float32)
    o = o.reshape(B_, S_, D_MODEL).astype(h.dtype) * jax.nn.silu(g_out)
    return o @ w_out


def headwise(x, w):
    B_, S_, _ = x.shape
    xg = x.reshape(B_, S_, w.shape[0], w.shape[1])
    return jnp.einsum('bsgi,gio->bsgo', xg, w).reshape(B_, S_, -1)


def mlstm_chunked(q, k, v, i_pre, f_pre):
    B_, S_, H, d = q.shape
    C = MLSTM_CHUNK
    N = S_ // C
    k = k * (d ** -0.5)
    log_f = jax.nn.log_sigmoid(f_pre)

    def to_chunks(t):
        return jnp.moveaxis(t.reshape((B_, N, C) + t.shape[2:]), 1, 0)

    qc, kc, vc, li, lf = map(to_chunks, (q, k, v, i_pre, log_f))
    bc = jnp.cumsum(lf, axis=2)
    causal = jnp.tril(jnp.ones((C, C), dtype=bool))[None, :, :, None]

    def step(carry, inp):
        Cm, n, m = carry
        qt, kt, vt, it, bt = inp
        log_w = bt[:, :, None, :] - bt[:, None, :, :] + it[:, None, :, :]
        log_w = jnp.where(causal, log_w, -jnp.inf)
        log_inter = bt + m[:, None, :]
        m_t = jnp.maximum(jnp.max(log_w, axis=2), log_inter)
        s_qk = jnp.einsum('bthd,bshd->btsh', qt, kt) * jnp.exp(log_w - m_t[:, :, None, :])
        w_inter = jnp.exp(log_inter - m_t)
        num = jnp.einsum('btsh,bshd->bthd', s_qk, vt) + w_inter[..., None] * jnp.einsum('bthd,bhde->bthe', qt, Cm)
        den = jnp.sum(s_qk, axis=2) + w_inter * jnp.einsum('bthd,bhd->bth', qt, n)
        h_out = num / jnp.maximum(jnp.abs(den), jnp.exp(-m_t))[..., None]
        b_last = bt[:, -1]
        log_ws = b_last[:, None, :] - bt + it
        m_new = jnp.maximum(b_last + m, jnp.max(log_ws, axis=1))
        ws = jnp.exp(log_ws - m_new[:, None, :])
        dec = jnp.exp(b_last + m - m_new)
        kw = kt * ws[..., None]
        Cm = dec[..., None, None] * Cm + jnp.einsum('bshd,bshe->bhde', kw, vt)
        n = dec[..., None] * n + jnp.sum(kw, axis=1)
        return (Cm, n, m_new), h_out

    init = (jnp.zeros((B_, H, d, d), jnp.float32), jnp.zeros((B_, H, d), jnp.float32), jnp.zeros((B_, H), jnp.float32))
    _, h_all = lax.scan(step, init, (qc, kc, vc, li, bc))
    return jnp.moveaxis(h_all, 0, 1).reshape(B_, S_, H, d)


def mlstm_layer(h, w_up, conv_w, conv_b, w_q, w_k, w_v, w_gates, b_gates, norm_g, skip, w_down):
    B_, S_, _ = h.shape
    xm, z = jnp.split(h @ w_up, 2, axis=-1)
    xc = jax.nn.silu(causal_dwconv(xm, conv_w, conv_b))
    q = headwise(xc, w_q)
    k = headwise(xc, w_k)
    v = headwise(xm, w_v)
    gates = jnp.concatenate([q, k, v], axis=-1) @ w_gates + b_gates
    i_pre, f_pre = jnp.split(gates.astype(jnp.float32), 2, axis=-1)
    heads = lambda t: t.reshape(B_, S_, MLSTM_HEADS, MLSTM_HDIM).astype(jnp.float32)
    hc = mlstm_chunked(heads(q), heads(k), heads(v), i_pre, f_pre)
    mu = jnp.mean(hc, axis=-1, keepdims=True)
    var = jnp.mean(jnp.square(hc - mu), axis=-1, keepdims=True)
    hc = ((hc - mu) * lax.rsqrt(var + LN_EPS)).reshape(B_, S_, MLSTM_INNER).astype(h.dtype)
    hc = hc * norm_g + skip * xc
    return (hc * jax.nn.silu(z)) @ w_down


def stick_breaking_layer(h, w_qkv, w_out):
    B_, S_, _ = h.shape
    qkv = (h @ w_qkv).reshape(B_, S_, 3, SB_HEADS, SB_HDIM).astype(jnp.float32)
    q = qkv[:, :, 0] * (SB_HDIM ** -0.5)
    k = qkv[:, :, 1]
    v = qkv[:, :, 2]
    blocks = []
    for t0 in range(0, S_, SB_BLOCK):
        t1 = t0 + SB_BLOCK
        z = jnp.einsum('bthd,bshd->bhts', q[:, t0:t1], k[:, :t1])
        valid = jnp.arange(t1)[None, :] < jnp.arange(t0, t1)[:, None]
        neg_log_1mb = jnp.where(valid, jax.nn.softplus(z), 0.0)
        later = lax.cumsum(neg_log_1mb, axis=3, reverse=True) - neg_log_1mb
        a = jnp.where(valid, jnp.exp(jax.nn.log_sigmoid(z) - later), 0.0)
        blocks.append(jnp.einsum('bhts,bshd->bthd', a, v[:, :t1]))
    o = jnp.concatenate(blocks, axis=1).reshape(B_, S_, D_MODEL).astype(h.dtype)
    return o @ w_out


def conv_ffn(h, w_up, conv_w, conv_b, w_down):
    u = causal_dwconv(h @ w_up, conv_w, conv_b)
    gate, val = jnp.split(u, 2, axis=-1)
    return (jax.nn.silu(gate) * val) @ w_down


def setup_inputs(seed: int = 0) -> dict:
    key = jax.random.key(seed)
    ks = iter(jax.random.split(key, 48))

    def nrm(shape, scale):
        return scale * jax.random.normal(next(ks), shape, jnp.float32)

    D, F = D_MODEL, D_FF
    NA, NB, NC, ND = N_CONV_LAYERS, N_HGRN_LAYERS, N_MLSTM_LAYERS, N_SB_LAYERS
    G = MLSTM_INNER // MLSTM_QKV_BLOCK
    blk = MLSTM_QKV_BLOCK
    b_gates = jnp.concatenate([nrm((NC, MLSTM_HEADS), 0.1),
                               jnp.linspace(3.0, 6.0, MLSTM_HEADS)[None, :] + nrm((NC, MLSTM_HEADS), 0.1)], axis=-1)
    return {
        'x': nrm((BATCH, SEQ, D), 1.0),
        'c': nrm((BATCH, D), 1.0),
        'ada_w': nrm((DEPTH, D, 6 * D), 0.1 * D ** -0.5),
        'ada_b': nrm((DEPTH, 6 * D), 0.02),
        'post_ln_g': 1.0 + nrm((DEPTH, 2, D), 0.05),
        'post_ln_b': nrm((DEPTH, 2, D), 0.02),
        'ffn_w_up': nrm((DEPTH, D, 2 * F), D ** -0.5),
        'ffn_conv_w': nrm((DEPTH, FFN_CONV, 2 * F), FFN_CONV ** -0.5),
        'ffn_conv_b': nrm((DEPTH, 2 * F), 0.02),
        'ffn_w_down': nrm((DEPTH, F, D), DN_BETA * F ** -0.5),
        'cc_w_pw1': nrm((NA, D, 2 * D), D ** -0.5),
        'cc_b_pw1': nrm((NA, 2 * D), 0.02),
        'cc_dw_w': nrm((NA, CONV_WIDTH, D), CONV_WIDTH ** -0.5),
        'cc_dw_b': nrm((NA, D), 0.02),
        'cc_ln_g': 1.0 + nrm((NA, D), 0.05),
        'cc_ln_b': nrm((NA, D), 0.02),
        'cc_w_pw2': nrm((NA, D, D), DN_BETA * D ** -0.5),
        'cc_b_pw2': nrm((NA, D), 0.02),
        'hg_lb_logits': 1.0 + nrm((DEPTH, HGRN_FDIM), 0.5),
        'hg_w_in': nrm((NB, D, 2 * HGRN_FDIM + 2 * D), D ** -0.5),
        'hg_norm_g': 1.0 + nrm((NB, HGRN_VDIM), 0.05),
        'hg_w_out': nrm((NB, D, D), DN_BETA * D ** -0.5),
        'ml_w_up': nrm((NC, D, 2 * MLSTM_INNER), D ** -0.5),
        'ml_conv_w': nrm((NC, MLSTM_CONV, MLSTM_INNER), MLSTM_CONV ** -0.5),
        'ml_conv_b': nrm((NC, MLSTM_INNER), 0.02),
        'ml_w_q': nrm((NC, G, blk, blk), blk ** -0.5),
        'ml_w_k': nrm((NC, G, blk, blk), blk ** -0.5),
        'ml_w_v': nrm((NC, G, blk, blk), blk ** -0.5),
        'ml_w_gates': nrm((NC, 3 * MLSTM_INNER, 2 * MLSTM_HEADS), 0.1 * (3 * MLSTM_INNER) ** -0.5),
        'ml_b_gates': b_gates,
        'ml_norm_g': 1.0 + nrm((NC, MLSTM_INNER), 0.05),
        'ml_skip': 1.0 + nrm((NC, MLSTM_INNER), 0.05),
        'ml_w_down': nrm((NC, MLSTM_INNER, D), DN_BETA * MLSTM_INNER ** -0.5),
        'sb_w_qkv': nrm((ND, D, 3 * D), D ** -0.5),
        'sb_w_out': nrm((ND, D, D), DN_BETA * D ** -0.5),
    }


def reference(x, c, ada_w, ada_b, post_ln_g, post_ln_b,
              ffn_w_up, ffn_conv_w, ffn_conv_b, ffn_w_down,
              cc_w_pw1, cc_b_pw1, cc_dw_w, cc_dw_b, cc_ln_g, cc_ln_b, cc_w_pw2, cc_b_pw2,
              hg_lb_logits, hg_w_in, hg_norm_g, hg_w_out,
              ml_w_up, ml_conv_w, ml_conv_b, ml_w_q, ml_w_k, ml_w_v, ml_w_gates, ml_b_gates,
              ml_norm_g, ml_skip, ml_w_down,
              sb_w_qkv, sb_w_out):
    lb_all = jax.nn.softmax(hg_lb_logits.astype(jnp.float32), axis=0)
    lb_all = jnp.cumsum(lb_all, axis=0) - lb_all[:1]
    for i in range(DEPTH):
        kind, j = i % N_MIXERS, i // N_MIXERS
        shift1, scale1, gate1, shift2, scale2, gate2 = adaln(c, ada_w[i], ada_b[i])
        h = x * (1.0 + scale1) + shift1
        if kind == 0:
            y = conformer_conv_module(h, cc_w_pw1[j], cc_b_pw1[j], cc_dw_w[j], cc_dw_b[j],
                                      cc_ln_g[j], cc_ln_b[j], cc_w_pw2[j], cc_b_pw2[j])
        elif kind == 1:
            y = hgrn2_layer(h, lb_all[i], hg_w_in[j], hg_norm_g[j], hg_w_out[j])
        elif kind == 2:
            y = mlstm_layer(h, ml_w_up[j], ml_conv_w[j], ml_conv_b[j], ml_w_q[j], ml_w_k[j], ml_w_v[j],
                            ml_w_gates[j], ml_b_gates[j], ml_norm_g[j], ml_skip[j], ml_w_down[j])
        else:
            y = stick_breaking_layer(h, sb_w_qkv[j], sb_w_out[j])
        x = layer_norm(DN_ALPHA * x + (1.0 + gate1) * y, post_ln_g[i, 0], post_ln_b[i, 0])
        h = x * (1.0 + scale2) + shift2
        y = conv_ffn(h, ffn_w_up[i], ffn_conv_w[i], ffn_conv_b[i], ffn_w_down[i])
        x = layer_norm(DN_ALPHA * x + (1.0 + gate2) * y, post_ln_g[i, 1], post_ln_b[i, 1])
    return x
```

```python
import functools

import jax
import jax.numpy as jnp
from jax import lax
from jax.experimental import pallas as pl
from jax.experimental.pallas import tpu as pltpu

F32 = jnp.float32
BF16 = jnp.bfloat16

D_MODEL = 1024
DEPTH = 4
D_FF = 2816
FFN_CONV = 3
CONV_WIDTH = 31
HGRN_HEADS = 8
HGRN_DK = 128
HGRN_CHUNK = 32
MLSTM_INNER = 2048
MLSTM_HEADS = 4
MLSTM_HDIM = 512
MLSTM_CONV = 4
MLSTM_QKV_BLOCK = 4
SB_HEADS = 16
SB_HDIM = 64
DN_ALPHA = (2.0 * DEPTH) ** 0.25
LN_EPS = 1e-5
RMS_EPS = 1e-6

VMEM_LIMIT_BYTES = 56 * 1024 * 1024
LANES = 128
SUBLANES = 8
MXU_DIM = 256

FFN_TM = 512
FFN_FC = 256
CONF_TM = 512
CONF_CC = 256
CONF_RB = 128
CONF_HALO = 32
HGRN_TM = 256
MLSTM_TM = 256
SB_T = 256


def _sigmoid(x):
    return 1.0 / (1.0 + jnp.exp(-x))


def _silu(x):
    return x * _sigmoid(x)


def _dot(a, b):
    return jnp.dot(a, b, preferred_element_type=F32)


def _dot_nt(a, b):
    return lax.dot_general(a, b, (((1,), (1,)), ((), ())), preferred_element_type=F32)


def _dot_tn(a, b):
    return lax.dot_general(a, b, (((0,), (0,)), ((), ())), preferred_element_type=F32)


def _split3(x):
    hi = x.astype(BF16)
    r1 = x - hi.astype(F32)
    mid = r1.astype(BF16)
    lo = (r1 - mid.astype(F32)).astype(BF16)
    return hi, mid, lo


def _ln_rows(r, g, b, eps):
    mu = jnp.mean(r, axis=-1, keepdims=True)
    d = r - mu
    var = jnp.mean(d * d, axis=-1, keepdims=True)
    return d * lax.rsqrt(var + eps) * g + b


def _modulate(x, mod_ref, which):
    base = 3 * which
    return x * (1.0 + mod_ref[base + 1:base + 2, :]) + mod_ref[base:base + 1, :]


def _post_norm(x, y, mod_ref, which, g_ref, b_ref):
    gate = mod_ref[3 * which + 2:3 * which + 3, :]
    return _ln_rows(DN_ALPHA * x + (1.0 + gate) * y, g_ref[...], b_ref[...], LN_EPS)


def _const_spec(shape):
    nd = len(shape)
    return pl.BlockSpec(shape, lambda *_: (0,) * nd, pipeline_mode=pl.Buffered(1))


def _tile_spec(tm):
    return pl.BlockSpec((None, tm, D_MODEL), lambda b, i: (b, i, 0))


def _mod_spec():
    return pl.BlockSpec((None, 6, D_MODEL), lambda b, i: (b, 0, 0))


def _params():
    return pltpu.CompilerParams(dimension_semantics=("arbitrary", "arbitrary"),
                                vmem_limit_bytes=VMEM_LIMIT_BYTES)


def _ada_kernel(c_ref, w_ref, b_ref, o_ref):
    o_ref[...] = _dot(_silu(c_ref[...]), w_ref[...]) + b_ref[...]


def _ada_call(c, ada_w, ada_b):
    depth, d, n = ada_w.shape
    bsz = c.shape[0]
    tn = 1536
    return pl.pallas_call(
        _ada_kernel,
        out_shape=jax.ShapeDtypeStruct((depth, bsz, n), F32),
        grid=(depth, n // tn),
        in_specs=[pl.BlockSpec((bsz, d), lambda l, j: (0, 0)),
                  pl.BlockSpec((None, d, tn), lambda l, j: (l, 0, j)),
                  pl.BlockSpec((None, 1, tn), lambda l, j: (l, 0, j))],
        out_specs=pl.BlockSpec((None, bsz, tn), lambda l, j: (l, 0, j)),
        compiler_params=_params(),
        name="ada_mod",
    )(c, ada_w, ada_b.reshape(depth, 1, n))


def _ffn_kernel(x_ref, mod_ref, wup_ref, cw_ref, cb_ref, wdn_ref, g_ref, b_ref, o_ref,
                ubuf, carry, acc_ref, *, tm, fc, nf):
    i = pl.program_id(1)

    @pl.when(i == 0)
    def _():
        carry[...] = jnp.zeros_like(carry)

    x = x_ref[...]
    h = _modulate(x, mod_ref, 1).astype(BF16)
    acc_ref[...] = jnp.zeros_like(acc_ref)

    def body(j, _):
        u = _dot(h, wup_ref[j])
        ubuf[0:SUBLANES, :] = carry[j]
        ubuf[SUBLANES:SUBLANES + tm, :] = u
        carry[j] = u[tm - SUBLANES:tm, :]
        y = (cw_ref[j, 2:3, :] * u
             + cw_ref[j, 1:2, :] * ubuf[SUBLANES - 1:SUBLANES - 1 + tm, :]
             + cw_ref[j, 0:1, :] * ubuf[SUBLANES - 2:SUBLANES - 2 + tm, :]
             + cb_ref[j])
        a = _silu(y[:, :fc]) * y[:, fc:]
        acc_ref[...] += _dot(a.astype(BF16), wdn_ref[j])
        return 0

    lax.fori_loop(0, nf, body, 0)
    o_ref[...] = _post_norm(x, acc_ref[...], mod_ref, 1, g_ref, b_ref)


def _ffn_call(x, mod, w_up, conv_w, conv_b, w_down, ln_g, ln_b):
    bsz, seq, d = x.shape
    f = w_down.shape[0]
    tm, fc = min(FFN_TM, seq), FFN_FC
    nf = f // fc
    wup = w_up.astype(BF16).reshape(d, 2, nf, fc).transpose(2, 0, 1, 3).reshape(nf, d, 2 * fc)
    cw = conv_w.reshape(FFN_CONV, 2, nf, fc).transpose(2, 0, 1, 3).reshape(nf, FFN_CONV, 2 * fc)
    cb = conv_b.reshape(2, nf, fc).transpose(1, 0, 2).reshape(nf, 1, 2 * fc)
    wdn = w_down.astype(BF16).reshape(nf, fc, d)
    kern = functools.partial(_ffn_kernel, tm=tm, fc=fc, nf=nf)
    return pl.pallas_call(
        kern,
        out_shape=jax.ShapeDtypeStruct(x.shape, F32),
        grid=(bsz, seq // tm),
        in_specs=[_tile_spec(tm), _mod_spec(),
                  _const_spec(wup.shape), _const_spec(cw.shape), _const_spec(cb.shape),
                  _const_spec(wdn.shape), _const_spec((1, d)), _const_spec((1, d))],
        out_specs=_tile_spec(tm),
        scratch_shapes=[pltpu.VMEM((SUBLANES + tm, 2 * fc), F32),
                        pltpu.VMEM((nf, SUBLANES, 2 * fc), F32),
                        pltpu.VMEM((tm, d), F32)],
        compiler_params=_params(),
        name="conv_ffn",
    )(x, mod, wup, cw, cb, wdn, ln_g.reshape(1, d), ln_b.reshape(1, d))


def _conf_kernel(x_ref, mod_ref, w1_ref, b1_ref, dw_ref, dwb_ref, lng_ref, lnb_ref, w2_ref, b2_ref,
                 g_ref, b_ref, o_ref, cbuf, ybuf, *, tm, cc, nc):
    i = pl.program_id(1)
    d = D_MODEL
    halo = CONF_HALO

    @pl.when(i == 0)
    def _():
        cbuf[:, 0:halo, :] = jnp.zeros((nc, halo, cc), F32)

    x = x_ref[...]
    h = _modulate(x, mod_ref, 0).astype(BF16)
    for c in range(nc):
        lo, hi = c * cc, (c + 1) * cc
        a = _dot(h, w1_ref[:, lo:hi]) + b1_ref[:, lo:hi]
        g = _dot(h, w1_ref[:, d + lo:d + hi]) + b1_ref[:, d + lo:d + hi]
        cbuf[c, halo:halo + tm, :] = a * _sigmoid(g)

    first = halo - (CONV_WIDTH - 1)

    def conv_body(c, _):
        for r0 in range(0, tm, CONF_RB):
            acc = jnp.broadcast_to(dwb_ref[c], (CONF_RB, cc))
            for k in range(CONV_WIDTH):
                acc = acc + dw_ref[c, k:k + 1, :] * cbuf[c, r0 + first + k:r0 + first + k + CONF_RB, :]
            ybuf[c, r0:r0 + CONF_RB, :] = acc
        cbuf[c, 0:halo, :] = cbuf[c, tm:tm + halo, :]
        return 0

    lax.fori_loop(0, nc, conv_body, 0)

    ys = [ybuf[c] for c in range(nc)]
    mu = sum(jnp.sum(y, axis=-1, keepdims=True) for y in ys) * (1.0 / d)
    var = sum(jnp.sum((y - mu) * (y - mu), axis=-1, keepdims=True) for y in ys) * (1.0 / d)
    rstd = lax.rsqrt(var + LN_EPS)
    out = jnp.broadcast_to(b2_ref[...], (tm, d))
    for c in range(nc):
        lo, hi = c * cc, (c + 1) * cc
        z = _silu((ys[c] - mu) * rstd * lng_ref[:, lo:hi] + lnb_ref[:, lo:hi])
        out = out + _dot(z.astype(BF16), w2_ref[lo:hi, :])
    o_ref[...] = _post_norm(x, out, mod_ref, 0, g_ref, b_ref)


def _conf_call(x, mod, w_pw1, b_pw1, dw_w, dw_b, ln_g, ln_b, w_pw2, b_pw2, pg, pb):
    bsz, seq, d = x.shape
    tm, cc = min(CONF_TM, seq), CONF_CC
    nc = d // cc
    dw = dw_w.reshape(CONV_WIDTH, nc, cc).transpose(1, 0, 2)
    dwb = dw_b.reshape(nc, 1, cc)
    kern = functools.partial(_conf_kernel, tm=tm, cc=cc, nc=nc)
    row = lambda v: v.reshape(1, -1)
    return pl.pallas_call(
        kern,
        out_shape=jax.ShapeDtypeStruct(x.shape, F32),
        grid=(bsz, seq // tm),
        in_specs=[_tile_spec(tm), _mod_spec(),
                  _const_spec((d, 2 * d)), _const_spec((1, 2 * d)),
                  _const_spec(dw.shape), _const_spec(dwb.shape),
                  _const_spec((1, d)), _const_spec((1, d)),
                  _const_spec((d, d)), _const_spec((1, d)),
                  _const_spec((1, d)), _const_spec((1, d))],
        out_specs=_tile_spec(tm),
        scratch_shapes=[pltpu.VMEM((nc, CONF_HALO + tm, cc), F32),
                        pltpu.VMEM((nc, tm, cc), F32)],
        compiler_params=_params(),
        name="conformer_conv",
    )(x, mod, w_pw1.astype(BF16), row(b_pw1), dw, dwb, row(ln_g), row(ln_b),
      w_pw2.astype(BF16), row(b_pw2), row(pg), row(pb))


def _hgrn_kernel(x_ref, mod_ref, lb_ref, win_ref, ng_ref, wout_ref, tri_ref, g_ref, b_ref, o_ref,
                 st_ref, obuf, *, tm, layer):
    i = pl.program_id(1)
    d = D_MODEL
    ch = HGRN_CHUNK
    nch = tm // ch

    @pl.when(i == 0)
    def _():
        st_ref[...] = jnp.zeros_like(st_ref)

    x = x_ref[...]
    h = _modulate(x, mod_ref, 0).astype(BF16)
    q = _silu(_dot(h, win_ref[:, 0:d]))
    logits = lb_ref[...]
    e = jnp.exp(logits - jnp.max(logits, axis=0, keepdims=True))
    sm = e / jnp.sum(e, axis=0, keepdims=True)
    lb = jnp.sum(sm[1:layer + 1, :], axis=0, keepdims=True) if layer > 0 else jnp.zeros((1, d), F32)
    f = lb + (1.0 - lb) * _sigmoid(_dot(h, win_ref[:, d:2 * d]))
    k = 1.0 - f
    lf = jnp.log(f)
    tri = tri_ref[...]
    bcum = sum(_dot(tri, part) for part in _split3(lf))
    b3 = bcum.reshape(nch, ch, d)
    b_mid = b3[:, ch // 2 - 1:ch // 2, :]
    b_last = b3[:, ch - 1:ch, :]
    q3 = q.reshape(nch, ch, d)
    k3 = k.reshape(nch, ch, d)
    qs = (q3 * jnp.exp(b3 - b_mid)).reshape(tm, d).astype(BF16)
    ks = (k3 * jnp.exp(b_mid - b3)).reshape(tm, d).astype(BF16)
    qd = (q3 * jnp.exp(b3)).reshape(tm, d).astype(BF16)
    kd = (k3 * jnp.exp(b_last - b3)).reshape(tm, d).astype(BF16)
    dec = jnp.exp(b_last)
    vb = _dot(h, win_ref[:, 2 * d:3 * d]).astype(BF16)

    causal = tri.astype(F32) > 0.5

    for hh in range(HGRN_HEADS):
        sl = slice(hh * HGRN_DK, (hh + 1) * HGRN_DK)
        sc = _dot_nt(qs[:, sl], ks[:, sl])
        o_intra = _dot(jnp.where(causal, sc, 0.0).astype(BF16), vb[:, sl])
        st = st_ref[hh]
        parts = []
        for c in range(nch):
            rows = slice(c * ch, (c + 1) * ch)
            parts.append(_dot_nt(qd[rows, sl], st.astype(BF16)))
            st = dec[c, :, sl] * st + _dot_tn(vb[rows, sl], kd[rows, sl])
        st_ref[hh] = st
        o = o_intra + jnp.concatenate(parts, axis=0)
        o = o * lax.rsqrt(jnp.mean(o * o, axis=-1, keepdims=True) + RMS_EPS) * ng_ref[...]
        obuf[:, sl] = o
    gate = _silu(_dot(h, win_ref[:, 3 * d:4 * d]))
    y = _dot((obuf[...] * gate).astype(BF16), wout_ref[...])
    o_ref[...] = _post_norm(x, y, mod_ref, 0, g_ref, b_ref)


def _hgrn_call(x, mod, layer, lb_logits, w_in, norm_g, w_out, pg, pb):
    bsz, seq, d = x.shape
    tm = min(HGRN_TM, seq)
    r = jnp.arange(tm)
    tri = ((r[:, None] // HGRN_CHUNK == r[None, :] // HGRN_CHUNK) & (r[None, :] <= r[:, None])).astype(BF16)
    kern = functools.partial(_hgrn_kernel, tm=tm, layer=layer)
    row = lambda v: v.reshape(1, -1)
    return pl.pallas_call(
        kern,
        out_shape=jax.ShapeDtypeStruct(x.shape, F32),
        grid=(bsz, seq // tm),
        in_specs=[_tile_spec(tm), _mod_spec(), _const_spec(lb_logits.shape),
                  _const_spec((d, 4 * d)), _const_spec((1, HGRN_DK)), _const_spec((d, d)),
                  _const_spec((tm, tm)), _const_spec((1, d)), _const_spec((1, d))],
        out_specs=_tile_spec(tm),
        scratch_shapes=[pltpu.VMEM((HGRN_HEADS, HGRN_DK, HGRN_DK), F32),
                        pltpu.VMEM((tm, d), F32)],
        compiler_params=_params(),
        name="hgrn2",
    )(x, mod, lb_logits.astype(F32), w_in.astype(BF16), row(norm_g), w_out.astype(BF16), tri, row(pg), row(pb))


def _mlstm_kernel(x_ref, mod_ref, wup_ref, cw_ref, cb_ref, wqk_ref, wv_ref, wg_ref, bg_ref,
                  ng_ref, sk_ref, wdn_ref, tri_ref, g_ref, b_ref, o_ref,
                  xbuf, qbuf, kbuf, vbuf, c_ref, n_ref, m_ref, hbuf, *, tm):
    i = pl.program_id(1)
    d = D_MODEL
    inner = MLSTM_INNER
    hd = MLSTM_HDIM
    nh = MLSTM_HEADS
    blk = MXU_DIM
    pad = SUBLANES

    @pl.when(i == 0)
    def _():
        c_ref[...] = jnp.zeros_like(c_ref)
        n_ref[...] = jnp.zeros_like(n_ref)
        m_ref[...] = jnp.zeros_like(m_ref)
        xbuf[0:pad, :] = jnp.zeros((pad, inner), F32)

    x = x_ref[...]
    h = _modulate(x, mod_ref, 0).astype(BF16)
    xm = _dot(h, wup_ref[:, 0:inner])
    xbuf[pad:pad + tm, :] = xm
    conv = cw_ref[MLSTM_CONV - 1:MLSTM_CONV, :] * xm + cb_ref[...]
    for kk in range(MLSTM_CONV - 1):
        off = pad - (MLSTM_CONV - 1) + kk
        conv = conv + cw_ref[kk:kk + 1, :] * xbuf[off:off + tm, :]
    xbuf[0:pad, :] = xm[tm - pad:tm, :]
    xc = _silu(conv)
    xcb = xc.astype(BF16)
    xmb = xm.astype(BF16)
    for j in range(inner // blk):
        cols = slice(j * blk, (j + 1) * blk)
        qk = _dot(xcb[:, cols], wqk_ref[j])
        qbuf[:, cols] = qk[:, 0:blk]
        kbuf[:, cols] = qk[:, blk:2 * blk]
        vbuf[:, cols] = _dot(xmb[:, cols], wv_ref[j])
    qf = qbuf[...]
    kf = kbuf[...]
    vf = vbuf[...]
    qb = qf.astype(BF16)
    vb = vf.astype(BF16)
    gates = (_dot(qb, wg_ref[0:inner, :]) + _dot(kf.astype(BF16), wg_ref[inner:2 * inner, :])
             + _dot(vb, wg_ref[2 * inner:3 * inner, :]) + bg_ref[...])
    log_f = jnp.minimum(gates, 0.0) - jnp.log(1.0 + jnp.exp(-jnp.abs(gates)))
    tri = tri_ref[...]
    bc = sum(_dot(tri, part) for part in _split3(log_f))
    gates_t = gates.T
    bc_t = bc.T
    row = lax.broadcasted_iota(jnp.int32, (tm, tm), 0)
    col = lax.broadcasted_iota(jnp.int32, (tm, tm), 1)
    causal = col <= row
    kscaled = kf * (hd ** -0.5)

    for hh in range(nh):
        sl = slice(hh * hd, (hh + 1) * hd)
        bt_col = bc[:, nh + hh:nh + hh + 1]
        it_col = gates[:, hh:hh + 1]
        bt_row = bc_t[nh + hh:nh + hh + 1, :]
        it_row = gates_t[hh:hh + 1, :]
        m_prev = m_ref[hh][0:1, 0:1]
        log_w = jnp.where(causal, bt_col - bt_row + it_row, -jnp.inf)
        log_inter = bt_col + m_prev
        m_t = jnp.maximum(jnp.max(log_w, axis=-1, keepdims=True), log_inter)
        kh = kscaled[:, sl]
        khb = kh.astype(BF16)
        qh = qf[:, sl]
        qhb = qb[:, sl]
        vhb = vb[:, sl]
        s_qk = _dot_nt(qhb, khb) * jnp.exp(log_w - m_t)
        w_inter = jnp.exp(log_inter - m_t)
        cm = c_ref[hh]
        nvec = n_ref[hh]
        num = _dot(s_qk.astype(BF16), vhb) + w_inter * _dot(qhb, cm.astype(BF16))
        den = jnp.sum(s_qk, axis=-1, keepdims=True) + w_inter * jnp.sum(qh * nvec, axis=-1, keepdims=True)
        h_out = num / jnp.maximum(jnp.abs(den), jnp.exp(-m_t))
        b_last = bt_col[tm - 1:tm, :]
        log_ws = b_last - bt_col + it_col
        m_new = jnp.maximum(b_last + m_prev, jnp.max(log_ws, axis=0, keepdims=True))
        ws = jnp.exp(log_ws - m_new)
        decay = jnp.exp(b_last + m_prev - m_new)
        kw = kh * ws
        c_ref[hh] = decay * cm + _dot_tn(kw.astype(BF16), vhb)
        n_ref[hh] = decay * nvec + jnp.sum(kw, axis=0, keepdims=True)
        m_ref[hh] = jnp.broadcast_to(m_new, (SUBLANES, LANES))
        mu = jnp.mean(h_out, axis=-1, keepdims=True)
        dlt = h_out - mu
        var = jnp.mean(dlt * dlt, axis=-1, keepdims=True)
        hn = dlt * lax.rsqrt(var + LN_EPS)
        z = _dot(h, wup_ref[:, inner + hh * hd:inner + (hh + 1) * hd])
        hbuf[:, sl] = ((hn * ng_ref[:, sl] + sk_ref[:, sl] * xc[:, sl]) * _silu(z)).astype(BF16)
    y = _dot(hbuf[...], wdn_ref[...])
    o_ref[...] = _post_norm(x, y, mod_ref, 0, g_ref, b_ref)


def _block_diag_tiles(w, tile):
    g, blk, _ = w.shape
    per = tile // blk
    w4 = w.reshape(g // per, per, blk, blk)
    eye = jnp.eye(per, dtype=w.dtype)
    t = jnp.einsum('npio,pq->npiqo', w4, eye)
    return t.reshape(g // per, tile, tile)


def _mlstm_call(x, mod, w_up, conv_w, conv_b, w_q, w_k, w_v, w_gates, b_gates, norm_g, skip, w_down, pg, pb):
    bsz, seq, d = x.shape
    tm = min(MLSTM_TM, seq)
    inner, nh, hd = MLSTM_INNER, MLSTM_HEADS, MLSTM_HDIM
    wq_t = _block_diag_tiles(w_q, MXU_DIM)
    wk_t = _block_diag_tiles(w_k, MXU_DIM)
    wqk = jnp.concatenate([wq_t, wk_t], axis=-1).astype(BF16)
    wv = _block_diag_tiles(w_v, MXU_DIM).astype(BF16)
    wg = jnp.pad(w_gates, ((0, 0), (0, LANES - 2 * nh))).astype(BF16)
    bg = jnp.pad(b_gates, (0, LANES - 2 * nh)).reshape(1, LANES)
    r = jnp.arange(tm)
    tri = (r[None, :] <= r[:, None]).astype(BF16)
    kern = functools.partial(_mlstm_kernel, tm=tm)
    row = lambda v: v.reshape(1, -1)
    return pl.pallas_call(
        kern,
        out_shape=jax.ShapeDtypeStruct(x.shape, F32),
        grid=(bsz, seq // tm),
        in_specs=[_tile_spec(tm), _mod_spec(),
                  _const_spec((d, 2 * inner)), _const_spec((MLSTM_CONV, inner)), _const_spec((1, inner)),
                  _const_spec(wqk.shape), _const_spec(wv.shape),
                  _const_spec(wg.shape), _const_spec((1, LANES)),
                  _const_spec((1, inner)), _const_spec((1, inner)), _const_spec((inner, d)),
                  _const_spec((tm, tm)), _const_spec((1, d)), _const_spec((1, d))],
        out_specs=_tile_spec(tm),
        scratch_shapes=[pltpu.VMEM((SUBLANES + tm, inner), F32),
                        pltpu.VMEM((tm, inner), F32), pltpu.VMEM((tm, inner), F32),
                        pltpu.VMEM((tm, inner), F32),
                        pltpu.VMEM((nh, hd, hd), F32), pltpu.VMEM((nh, 1, hd), F32),
                        pltpu.VMEM((nh, SUBLANES, LANES), F32),
                        pltpu.VMEM((tm, inner), BF16)],
        compiler_params=_params(),
        name="mlstm",
    )(x, mod, w_up.astype(BF16), conv_w, row(conv_b), wqk, wv, wg, bg,
      row(norm_g), row(skip), w_down.astype(BF16), tri, row(pg), row(pb))


def _sb_kernel(x_ref, mod_ref, wqkv_ref, wout_ref, g_ref, b_ref, o_ref, k_scr, v_scr, obuf, *, t):
    i = pl.program_id(1)
    d = D_MODEL
    x = x_ref[...]
    h = _modulate(x, mod_ref, 0).astype(BF16)
    q = (_dot(h, wqkv_ref[:, 0:d]) * (SB_HDIM ** -0.5)).astype(BF16)
    base = pl.multiple_of(i * t, t)
    k_scr[pl.ds(base, t), :] = _dot(h, wqkv_ref[:, d:2 * d]).astype(BF16)
    v_scr[pl.ds(base, t), :] = _dot(h, wqkv_ref[:, 2 * d:3 * d]).astype(BF16)

    row = lax.broadcasted_iota(jnp.int32, (t, t), 0)
    col = lax.broadcasted_iota(jnp.int32, (t, t), 1)
    strict = col < row
    upper = jnp.where(row > col, 1.0, 0.0).astype(BF16)
    lane = lax.broadcasted_iota(jnp.int32, (1, LANES), 1)
    head_masks = (lane < SB_HDIM, lane >= SB_HDIM)

    def block(qm, kb, vm, valid, acc, later0):
        z = _dot_nt(qm, kb)
        tail = jnp.log(1.0 + jnp.exp(-jnp.abs(z)))
        sp = jnp.maximum(z, 0.0) + tail
        ls = jnp.minimum(z, 0.0) - tail
        if valid is not None:
            sp = jnp.where(valid, sp, 0.0)
        sp_hi = sp.astype(BF16)
        sp_lo = (sp - sp_hi.astype(F32)).astype(BF16)
        later = _dot(sp_hi, upper) + _dot(sp_lo, upper) + later0
        a = jnp.exp(ls - later)
        if valid is not None:
            a = jnp.where(valid, a, 0.0)
        acc = acc + _dot(a.astype(BF16), vm)
        return acc, later0 + jnp.sum(sp, axis=-1, keepdims=True)

    for hp in range(SB_HEADS * SB_HDIM // LANES):
        sl = slice(hp * LANES, (hp + 1) * LANES)
        q2 = q[:, sl]
        qms = [jnp.where(mk, q2, jnp.zeros_like(q2)) for mk in head_masks]

        def kv_block(j):
            start = pl.multiple_of(j * t, t)
            kb = k_scr[pl.ds(start, t), sl]
            vb = v_scr[pl.ds(start, t), sl]
            return kb, [jnp.where(mk, vb, jnp.zeros_like(vb)) for mk in head_masks]

        kb, vms = kv_block(i)
        state = []
        for s in range(2):
            state.extend(block(qms[s], kb, vms[s], strict, jnp.zeros((t, LANES), F32), jnp.zeros((t, 1), F32)))

        def body(jj, carry):
            kb, vms = kv_block(i - jj)
            out = []
            for s in range(2):
                out.extend(block(qms[s], kb, vms[s], None, carry[2 * s], carry[2 * s + 1]))
            return tuple(out)

        state = lax.fori_loop(1, i + 1, body, tuple(state))
        obuf[:, sl] = (state[0] + state[2]).astype(BF16)
    y = _dot(obuf[...], wout_ref[...])
    o_ref[...] = _post_norm(x, y, mod_ref, 0, g_ref, b_ref)


def _sb_call(x, mod, w_qkv, w_out, pg, pb):
    bsz, seq, d = x.shape
    t = min(SB_T, seq)
    kern = functools.partial(_sb_kernel, t=t)
    row = lambda v: v.reshape(1, -1)
    return pl.pallas_call(
        kern,
        out_shape=jax.ShapeDtypeStruct(x.shape, F32),
        grid=(bsz, seq // t),
        in_specs=[_tile_spec(t), _mod_spec(), _const_spec((d, 3 * d)), _const_spec((d, d)),
                  _const_spec((1, d)), _const_spec((1, d))],
        out_specs=_tile_spec(t),
        scratch_shapes=[pltpu.VMEM((seq, d), BF16), pltpu.VMEM((seq, d), BF16),
                        pltpu.VMEM((t, d), BF16)],
        compiler_params=_params(),
        name="stick_breaking",
    )(x, mod, w_qkv.astype(BF16), w_out.astype(BF16), row(pg), row(pb))


def kernel(x, c, ada_w, ada_b, post_ln_g, post_ln_b, ffn_w_up, ffn_conv_w, ffn_conv_b, ffn_w_down, cc_w_pw1, cc_b_pw1, cc_dw_w, cc_dw_b, cc_ln_g, cc_ln_b, cc_w_pw2, cc_b_pw2, hg_lb_logits, hg_w_in, hg_norm_g, hg_w_out, ml_w_up, ml_conv_w, ml_conv_b, ml_w_q, ml_w_k, ml_w_v, ml_w_gates, ml_b_gates, ml_norm_g, ml_skip, ml_w_down, sb_w_qkv, sb_w_out):
    bsz = x.shape[0]
    mods = _ada_call(c, ada_w, ada_b).reshape(DEPTH, bsz, 6, D_MODEL)
    for i in range(DEPTH):
        kind, j = i % 4, i // 4
        mod = mods[i]
        pg, pb = post_ln_g[i, 0], post_ln_b[i, 0]
        if kind == 0:
            x = _conf_call(x, mod, cc_w_pw1[j], cc_b_pw1[j], cc_dw_w[j], cc_dw_b[j], cc_ln_g[j], cc_ln_b[j],
                           cc_w_pw2[j], cc_b_pw2[j], pg, pb)
        elif kind == 1:
            x = _hgrn_call(x, mod, i, hg_lb_logits, hg_w_in[j], hg_norm_g[j], hg_w_out[j], pg, pb)
        elif kind == 2:
            x = _mlstm_call(x, mod, ml_w_up[j], ml_conv_w[j], ml_conv_b[j], ml_w_q[j], ml_w_k[j], ml_w_v[j],
                            ml_w_gates[j], ml_b_gates[j], ml_norm_g[j], ml_skip[j], ml_w_down[j], pg, pb)
        else:
            x = _sb_call(x, mod, sb_w_qkv[j], sb_w_out[j], pg, pb)
        x = _ffn_call(x, mod, ffn_w_up[i], ffn_conv_w[i], ffn_conv_b[i], ffn_w_down[i],
                      post_ln_g[i, 1], post_ln_b[i, 1])
    return x
```

```python
import functools

import jax
import jax.numpy as jnp
from jax import lax
from jax.experimental import pallas as pl
from jax.experimental.pallas import tpu as pltpu

F32 = jnp.float32
BF16 = jnp.bfloat16

D_MODEL = 1024
DEPTH = 4
D_FF = 2816
FFN_CONV = 3
CONV_WIDTH = 31
HGRN_HEADS = 8
HGRN_DK = 128
HGRN_CHUNK = 32
MLSTM_INNER = 2048
MLSTM_HEADS = 4
MLSTM_HDIM = 512
MLSTM_CONV = 4
MLSTM_QKV_BLOCK = 4
SB_HEADS = 16
SB_HDIM = 64
LOG2_E = 1.4426950408889634
SB_MASKED = -1e30
DN_ALPHA = (2.0 * DEPTH) ** 0.25
LN_EPS = 1e-5
RMS_EPS = 1e-6

VMEM_LIMIT_BYTES = 56 * 1024 * 1024
LANES = 128
SUBLANES = 8
MXU_DIM = 256

FFN_TM = 512
FFN_FC = 256
CONF_TM = 512
CONF_CC = 256
CONF_RB = 128
CONF_HALO = 32
HGRN_TM = 256
MLSTM_TM = 256
SB_T = 256


def _sigmoid(x):
    return 1.0 / (1.0 + jnp.exp(-x))


def _silu(x):
    return x * _sigmoid(x)


def _dot(a, b):
    return jnp.dot(a, b, preferred_element_type=F32)


def _dot_nt(a, b):
    return lax.dot_general(a, b, (((1,), (1,)), ((), ())), preferred_element_type=F32)


def _dot_tn(a, b):
    return lax.dot_general(a, b, (((0,), (0,)), ((), ())), preferred_element_type=F32)


def _split3(x):
    hi = x.astype(BF16)
    r1 = x - hi.astype(F32)
    mid = r1.astype(BF16)
    lo = (r1 - mid.astype(F32)).astype(BF16)
    return hi, mid, lo


def _ln_rows(r, g, b, eps):
    mu = jnp.mean(r, axis=-1, keepdims=True)
    d = r - mu
    var = jnp.mean(d * d, axis=-1, keepdims=True)
    return d * lax.rsqrt(var + eps) * g + b


def _modulate(x, mod_ref, which):
    base = 3 * which
    return x * (1.0 + mod_ref[base + 1:base + 2, :]) + mod_ref[base:base + 1, :]


def _post_norm(x, y, mod_ref, which, g_ref, b_ref):
    gate = mod_ref[3 * which + 2:3 * which + 3, :]
    return _ln_rows(DN_ALPHA * x + (1.0 + gate) * y, g_ref[...], b_ref[...], LN_EPS)


def _const_spec(shape):
    nd = len(shape)
    return pl.BlockSpec(shape, lambda *_: (0,) * nd, pipeline_mode=pl.Buffered(1))


def _tile_spec(tm):
    return pl.BlockSpec((None, tm, D_MODEL), lambda b, i: (b, i, 0))


def _mod_spec():
    return pl.BlockSpec((None, 6, D_MODEL), lambda b, i: (b, 0, 0))


def _params():
    return pltpu.CompilerParams(dimension_semantics=("arbitrary", "arbitrary"),
                                vmem_limit_bytes=VMEM_LIMIT_BYTES)


def _ada_kernel(c_ref, w_ref, b_ref, o_ref):
    o_ref[...] = _dot(_silu(c_ref[...]), w_ref[...]) + b_ref[...]


def _ada_call(c, ada_w, ada_b):
    depth, d, n = ada_w.shape
    bsz = c.shape[0]
    tn = 1536
    return pl.pallas_call(
        _ada_kernel,
        out_shape=jax.ShapeDtypeStruct((depth, bsz, n), F32),
        grid=(depth, n // tn),
        in_specs=[pl.BlockSpec((bsz, d), lambda l, j: (0, 0)),
                  pl.BlockSpec((None, d, tn), lambda l, j: (l, 0, j)),
                  pl.BlockSpec((None, 1, tn), lambda l, j: (l, 0, j))],
        out_specs=pl.BlockSpec((None, bsz, tn), lambda l, j: (l, 0, j)),
        compiler_params=_params(),
        name="ada_mod",
    )(c, ada_w, ada_b.reshape(depth, 1, n))


def _ffn_kernel(x_ref, mod_ref, wup_ref, cw_ref, cb_ref, wdn_ref, g_ref, b_ref, o_ref,
                hbuf, ubuf_a, ubuf_b, carry, acc_ref, *, tm, fc, nf):
    i = pl.program_id(1)

    @pl.when(i == 0)
    def _():
        carry[...] = jnp.zeros_like(carry)

    x = x_ref[...]
    hbuf[...] = _modulate(x, mod_ref, 1).astype(BF16)
    acc_ref[...] = jnp.zeros_like(acc_ref)

    def up(j, ubuf):
        ubuf[0:SUBLANES, :] = carry[j]
        ubuf[SUBLANES:SUBLANES + tm, :] = _dot(hbuf[...], wup_ref[j])

    def down(j, ubuf):
        u = ubuf[SUBLANES:SUBLANES + tm, :]
        carry[j] = ubuf[tm:tm + SUBLANES, :]
        y = (cw_ref[j, 2:3, :] * u
             + cw_ref[j, 1:2, :] * ubuf[SUBLANES - 1:SUBLANES - 1 + tm, :]
             + cw_ref[j, 0:1, :] * ubuf[SUBLANES - 2:SUBLANES - 2 + tm, :]
             + cb_ref[j])
        a = _silu(y[:, :fc]) * y[:, fc:]
        acc_ref[...] += _dot(a.astype(BF16), wdn_ref[j])

    up(0, ubuf_a)

    def body(p, _):
        j = 2 * p
        up(j + 1, ubuf_b)
        down(j, ubuf_a)
        up(j + 2, ubuf_a)
        down(j + 1, ubuf_b)
        return 0

    lax.fori_loop(0, (nf - 1) // 2, body, 0)
    if nf % 2 == 0:
        up(nf - 1, ubuf_b)
        down(nf - 2, ubuf_a)
        down(nf - 1, ubuf_b)
    else:
        down(nf - 1, ubuf_a)
    o_ref[...] = _post_norm(x, acc_ref[...], mod_ref, 1, g_ref, b_ref)


def _ffn_call(x, mod, w_up, conv_w, conv_b, w_down, ln_g, ln_b):
    bsz, seq, d = x.shape
    f = w_down.shape[0]
    tm, fc = min(FFN_TM, seq), FFN_FC
    nf = f // fc
    wup = w_up.astype(BF16).reshape(d, 2, nf, fc).transpose(2, 0, 1, 3).reshape(nf, d, 2 * fc)
    cw = conv_w.reshape(FFN_CONV, 2, nf, fc).transpose(2, 0, 1, 3).reshape(nf, FFN_CONV, 2 * fc)
    cb = conv_b.reshape(2, nf, fc).transpose(1, 0, 2).reshape(nf, 1, 2 * fc)
    wdn = w_down.astype(BF16).reshape(nf, fc, d)
    kern = functools.partial(_ffn_kernel, tm=tm, fc=fc, nf=nf)
    return pl.pallas_call(
        kern,
        out_shape=jax.ShapeDtypeStruct(x.shape, F32),
        grid=(bsz, seq // tm),
        in_specs=[_tile_spec(tm), _mod_spec(),
                  _const_spec(wup.shape), _const_spec(cw.shape), _const_spec(cb.shape),
                  _const_spec(wdn.shape), _const_spec((1, d)), _const_spec((1, d))],
        out_specs=_tile_spec(tm),
        scratch_shapes=[pltpu.VMEM((tm, d), BF16),
                        pltpu.VMEM((SUBLANES + tm, 2 * fc), F32),
                        pltpu.VMEM((SUBLANES + tm, 2 * fc), F32),
                        pltpu.VMEM((nf, SUBLANES, 2 * fc), F32),
                        pltpu.VMEM((tm, d), F32)],
        compiler_params=_params(),
        name="conv_ffn",
    )(x, mod, wup, cw, cb, wdn, ln_g.reshape(1, d), ln_b.reshape(1, d))


def _conf_kernel(x_ref, mod_ref, w1_ref, b1_ref, dw_ref, dwb_ref, lng_ref, lnb_ref, w2_ref, b2_ref,
                 g_ref, b_ref, o_ref, cbuf, ybuf, *, tm, cc, nc):
    i = pl.program_id(1)
    d = D_MODEL
    halo = CONF_HALO

    @pl.when(i == 0)
    def _():
        cbuf[:, 0:halo, :] = jnp.zeros((nc, halo, cc), F32)

    x = x_ref[...]
    h = _modulate(x, mod_ref, 0).astype(BF16)
    for c in range(nc):
        lo, hi = c * cc, (c + 1) * cc
        a = _dot(h, w1_ref[:, lo:hi]) + b1_ref[:, lo:hi]
        g = _dot(h, w1_ref[:, d + lo:d + hi]) + b1_ref[:, d + lo:d + hi]
        cbuf[c, halo:halo + tm, :] = a * _sigmoid(g)

    first = halo - (CONV_WIDTH - 1)

    def conv_body(c, _):
        for r0 in range(0, tm, CONF_RB):
            acc = jnp.broadcast_to(dwb_ref[c], (CONF_RB, cc))
            for k in range(CONV_WIDTH):
                acc = acc + dw_ref[c, k:k + 1, :] * cbuf[c, r0 + first + k:r0 + first + k + CONF_RB, :]
            ybuf[c, r0:r0 + CONF_RB, :] = acc
        cbuf[c, 0:halo, :] = cbuf[c, tm:tm + halo, :]
        return 0

    lax.fori_loop(0, nc, conv_body, 0)

    ys = [ybuf[c] for c in range(nc)]
    mu = sum(jnp.sum(y, axis=-1, keepdims=True) for y in ys) * (1.0 / d)
    var = sum(jnp.sum((y - mu) * (y - mu), axis=-1, keepdims=True) for y in ys) * (1.0 / d)
    rstd = lax.rsqrt(var + LN_EPS)
    out = jnp.broadcast_to(b2_ref[...], (tm, d))
    for c in range(nc):
        lo, hi = c * cc, (c + 1) * cc
        z = _silu((ys[c] - mu) * rstd * lng_ref[:, lo:hi] + lnb_ref[:, lo:hi])
        out = out + _dot(z.astype(BF16), w2_ref[lo:hi, :])
    o_ref[...] = _post_norm(x, out, mod_ref, 0, g_ref, b_ref)


def _conf_call(x, mod, w_pw1, b_pw1, dw_w, dw_b, ln_g, ln_b, w_pw2, b_pw2, pg, pb):
    bsz, seq, d = x.shape
    tm, cc = min(CONF_TM, seq), CONF_CC
    nc = d // cc
    dw = dw_w.reshape(CONV_WIDTH, nc, cc).transpose(1, 0, 2)
    dwb = dw_b.reshape(nc, 1, cc)
    kern = functools.partial(_conf_kernel, tm=tm, cc=cc, nc=nc)
    row = lambda v: v.reshape(1, -1)
    return pl.pallas_call(
        kern,
        out_shape=jax.ShapeDtypeStruct(x.shape, F32),
        grid=(bsz, seq // tm),
        in_specs=[_tile_spec(tm), _mod_spec(),
                  _const_spec((d, 2 * d)), _const_spec((1, 2 * d)),
                  _const_spec(dw.shape), _const_spec(dwb.shape),
                  _const_spec((1, d)), _const_spec((1, d)),
                  _const_spec((d, d)), _const_spec((1, d)),
                  _const_spec((1, d)), _const_spec((1, d))],
        out_specs=_tile_spec(tm),
        scratch_shapes=[pltpu.VMEM((nc, CONF_HALO + tm, cc), F32),
                        pltpu.VMEM((nc, tm, cc), F32)],
        compiler_params=_params(),
        name="conformer_conv",
    )(x, mod, w_pw1.astype(BF16), row(b_pw1), dw, dwb, row(ln_g), row(ln_b),
      w_pw2.astype(BF16), row(b_pw2), row(pg), row(pb))


def _hgrn_kernel(x_ref, mod_ref, lb_ref, win_ref, ng_ref, wout_ref, tri_ref, g_ref, b_ref, o_ref,
                 st_ref, obuf, *, tm, layer):
    i = pl.program_id(1)
    d = D_MODEL
    ch = HGRN_CHUNK
    nch = tm // ch

    @pl.when(i == 0)
    def _():
        st_ref[...] = jnp.zeros_like(st_ref)

    x = x_ref[...]
    h = _modulate(x, mod_ref, 0).astype(BF16)
    q = _silu(_dot(h, win_ref[:, 0:d]))
    logits = lb_ref[...]
    e = jnp.exp(logits - jnp.max(logits, axis=0, keepdims=True))
    sm = e / jnp.sum(e, axis=0, keepdims=True)
    lb = jnp.sum(sm[1:layer + 1, :], axis=0, keepdims=True) if layer > 0 else jnp.zeros((1, d), F32)
    f = lb + (1.0 - lb) * _sigmoid(_dot(h, win_ref[:, d:2 * d]))
    k = 1.0 - f
    lf = jnp.log(f)
    tri = tri_ref[...]
    bcum = sum(_dot(tri, part) for part in _split3(lf))
    b3 = bcum.reshape(nch, ch, d)
    b_mid = b3[:, ch // 2 - 1:ch // 2, :]
    b_last = b3[:, ch - 1:ch, :]
    q3 = q.reshape(nch, ch, d)
    k3 = k.reshape(nch, ch, d)
    qs = (q3 * jnp.exp(b3 - b_mid)).reshape(tm, d).astype(BF16)
    ks = (k3 * jnp.exp(b_mid - b3)).reshape(tm, d).astype(BF16)
    qd = (q3 * jnp.exp(b3)).reshape(tm, d).astype(BF16)
    kd = (k3 * jnp.exp(b_last - b3)).reshape(tm, d).astype(BF16)
    dec = jnp.exp(b_last)
    vb = _dot(h, win_ref[:, 2 * d:3 * d]).astype(BF16)

    causal = tri.astype(F32) > 0.5

    for hh in range(HGRN_HEADS):
        sl = slice(hh * HGRN_DK, (hh + 1) * HGRN_DK)
        sc = _dot_nt(qs[:, sl], ks[:, sl])
        o_intra = _dot(jnp.where(causal, sc, 0.0).astype(BF16), vb[:, sl])
        st = st_ref[hh]
        parts = []
        for c in range(nch):
            rows = slice(c * ch, (c + 1) * ch)
            parts.append(_dot_nt(qd[rows, sl], st.astype(BF16)))
            st = dec[c, :, sl] * st + _dot_tn(vb[rows, sl], kd[rows, sl])
        st_ref[hh] = st
        o = o_intra + jnp.concatenate(parts, axis=0)
        o = o * lax.rsqrt(jnp.mean(o * o, axis=-1, keepdims=True) + RMS_EPS) * ng_ref[...]
        obuf[:, sl] = o
    gate = _silu(_dot(h, win_ref[:, 3 * d:4 * d]))
    y = _dot((obuf[...] * gate).astype(BF16), wout_ref[...])
    o_ref[...] = _post_norm(x, y, mod_ref, 0, g_ref, b_ref)


def _hgrn_call(x, mod, layer, lb_logits, w_in, norm_g, w_out, pg, pb):
    bsz, seq, d = x.shape
    tm = min(HGRN_TM, seq)
    r = jnp.arange(tm)
    tri = ((r[:, None] // HGRN_CHUNK == r[None, :] // HGRN_CHUNK) & (r[None, :] <= r[:, None])).astype(BF16)
    kern = functools.partial(_hgrn_kernel, tm=tm, layer=layer)
    row = lambda v: v.reshape(1, -1)
    return pl.pallas_call(
        kern,
        out_shape=jax.ShapeDtypeStruct(x.shape, F32),
        grid=(bsz, seq // tm),
        in_specs=[_tile_spec(tm), _mod_spec(), _const_spec(lb_logits.shape),
                  _const_spec((d, 4 * d)), _const_spec((1, HGRN_DK)), _const_spec((d, d)),
                  _const_spec((tm, tm)), _const_spec((1, d)), _const_spec((1, d))],
        out_specs=_tile_spec(tm),
        scratch_shapes=[pltpu.VMEM((HGRN_HEADS, HGRN_DK, HGRN_DK), F32),
                        pltpu.VMEM((tm, d), F32)],
        compiler_params=_params(),
        name="hgrn2",
    )(x, mod, lb_logits.astype(F32), w_in.astype(BF16), row(norm_g), w_out.astype(BF16), tri, row(pg), row(pb))


def _mlstm_kernel(x_ref, mod_ref, wup_ref, cw_ref, cb_ref, wqk_ref, wv_ref, wg_ref, bg_ref,
                  ng_ref, sk_ref, wdn_ref, tri_ref, g_ref, b_ref, o_ref,
                  xbuf, qbuf, kbuf, vbuf, c_ref, n_ref, m_ref, hbuf, *, tm):
    i = pl.program_id(1)
    d = D_MODEL
    inner = MLSTM_INNER
    hd = MLSTM_HDIM
    nh = MLSTM_HEADS
    blk = MXU_DIM
    pad = SUBLANES

    @pl.when(i == 0)
    def _():
        c_ref[...] = jnp.zeros_like(c_ref)
        n_ref[...] = jnp.zeros_like(n_ref)
        m_ref[...] = jnp.zeros_like(m_ref)
        xbuf[0:pad, :] = jnp.zeros((pad, inner), F32)

    x = x_ref[...]
    h = _modulate(x, mod_ref, 0).astype(BF16)
    xm = _dot(h, wup_ref[:, 0:inner])
    xbuf[pad:pad + tm, :] = xm
    conv = cw_ref[MLSTM_CONV - 1:MLSTM_CONV, :] * xm + cb_ref[...]
    for kk in range(MLSTM_CONV - 1):
        off = pad - (MLSTM_CONV - 1) + kk
        conv = conv + cw_ref[kk:kk + 1, :] * xbuf[off:off + tm, :]
    xbuf[0:pad, :] = xm[tm - pad:tm, :]
    xc = _silu(conv)
    xcb = xc.astype(BF16)
    xmb = xm.astype(BF16)
    for j in range(inner // blk):
        cols = slice(j * blk, (j + 1) * blk)
        qk = _dot(xcb[:, cols], wqk_ref[j])
        qbuf[:, cols] = qk[:, 0:blk]
        kbuf[:, cols] = qk[:, blk:2 * blk]
        vbuf[:, cols] = _dot(xmb[:, cols], wv_ref[j])
    qf = qbuf[...]
    kf = kbuf[...]
    vf = vbuf[...]
    qb = qf.astype(BF16)
    vb = vf.astype(BF16)
    gates = (_dot(qb, wg_ref[0:inner, :]) + _dot(kf.astype(BF16), wg_ref[inner:2 * inner, :])
             + _dot(vb, wg_ref[2 * inner:3 * inner, :]) + bg_ref[...])
    log_f = jnp.minimum(gates, 0.0) - jnp.log(1.0 + jnp.exp(-jnp.abs(gates)))
    tri = tri_ref[...]
    bc = sum(_dot(tri, part) for part in _split3(log_f))
    gates_t = gates.T
    bc_t = bc.T
    row = lax.broadcasted_iota(jnp.int32, (tm, tm), 0)
    col = lax.broadcasted_iota(jnp.int32, (tm, tm), 1)
    causal = col <= row
    kscaled = kf * (hd ** -0.5)

    for hh in range(nh):
        sl = slice(hh * hd, (hh + 1) * hd)
        bt_col = bc[:, nh + hh:nh + hh + 1]
        it_col = gates[:, hh:hh + 1]
        bt_row = bc_t[nh + hh:nh + hh + 1, :]
        it_row = gates_t[hh:hh + 1, :]
        m_prev = m_ref[hh][0:1, 0:1]
        log_w = jnp.where(causal, bt_col - bt_row + it_row, -jnp.inf)
        log_inter = bt_col + m_prev
        m_t = jnp.maximum(jnp.max(log_w, axis=-1, keepdims=True), log_inter)
        kh = kscaled[:, sl]
        khb = kh.astype(BF16)
        qh = qf[:, sl]
        qhb = qb[:, sl]
        vhb = vb[:, sl]
        s_qk = _dot_nt(qhb, khb) * jnp.exp(log_w - m_t)
        w_inter = jnp.exp(log_inter - m_t)
        cm = c_ref[hh]
        nvec = n_ref[hh]
        num = _dot(s_qk.astype(BF16), vhb) + w_inter * _dot(qhb, cm.astype(BF16))
        den = jnp.sum(s_qk, axis=-1, keepdims=True) + w_inter * jnp.sum(qh * nvec, axis=-1, keepdims=True)
        h_out = num / jnp.maximum(jnp.abs(den), jnp.exp(-m_t))
        b_last = bt_col[tm - 1:tm, :]
        log_ws = b_last - bt_col + it_col
        m_new = jnp.maximum(b_last + m_prev, jnp.max(log_ws, axis=0, keepdims=True))
        ws = jnp.exp(log_ws - m_new)
        decay = jnp.exp(b_last + m_prev - m_new)
        kw = kh * ws
        c_ref[hh] = decay * cm + _dot_tn(kw.astype(BF16), vhb)
        n_ref[hh] = decay * nvec + jnp.sum(kw, axis=0, keepdims=True)
        m_ref[hh] = jnp.broadcast_to(m_new, (SUBLANES, LANES))
        mu = jnp.mean(h_out, axis=-1, keepdims=True)
        dlt = h_out - mu
        var = jnp.mean(dlt * dlt, axis=-1, keepdims=True)
        hn = dlt * lax.rsqrt(var + LN_EPS)
        z = _dot(h, wup_ref[:, inner + hh * hd:inner + (hh + 1) * hd])
        hbuf[:, sl] = ((hn * ng_ref[:, sl] + sk_ref[:, sl] * xc[:, sl]) * _silu(z)).astype(BF16)
    y = _dot(hbuf[...], wdn_ref[...])
    o_ref[...] = _post_norm(x, y, mod_ref, 0, g_ref, b_ref)


def _block_diag_tiles(w, tile):
    g, blk, _ = w.shape
    per = tile // blk
    w4 = w.reshape(g // per, per, blk, blk)
    eye = jnp.eye(per, dtype=w.dtype)
    t = jnp.einsum('npio,pq->npiqo', w4, eye)
    return t.reshape(g // per, tile, tile)


def _mlstm_call(x, mod, w_up, conv_w, conv_b, w_q, w_k, w_v, w_gates, b_gates, norm_g, skip, w_down, pg, pb):
    bsz, seq, d = x.shape
    tm = min(MLSTM_TM, seq)
    inner, nh, hd = MLSTM_INNER, MLSTM_HEADS, MLSTM_HDIM
    wq_t = _block_diag_tiles(w_q, MXU_DIM)
    wk_t = _block_diag_tiles(w_k, MXU_DIM)
    wqk = jnp.concatenate([wq_t, wk_t], axis=-1).astype(BF16)
    wv = _block_diag_tiles(w_v, MXU_DIM).astype(BF16)
    wg = jnp.pad(w_gates, ((0, 0), (0, LANES - 2 * nh))).astype(BF16)
    bg = jnp.pad(b_gates, (0, LANES - 2 * nh)).reshape(1, LANES)
    r = jnp.arange(tm)
    tri = (r[None, :] <= r[:, None]).astype(BF16)
    kern = functools.partial(_mlstm_kernel, tm=tm)
    row = lambda v: v.reshape(1, -1)
    return pl.pallas_call(
        kern,
        out_shape=jax.ShapeDtypeStruct(x.shape, F32),
        grid=(bsz, seq // tm),
        in_specs=[_tile_spec(tm), _mod_spec(),
                  _const_spec((d, 2 * inner)), _const_spec((MLSTM_CONV, inner)), _const_spec((1, inner)),
                  _const_spec(wqk.shape), _const_spec(wv.shape),
                  _const_spec(wg.shape), _const_spec((1, LANES)),
                  _const_spec((1, inner)), _const_spec((1, inner)), _const_spec((inner, d)),
                  _const_spec((tm, tm)), _const_spec((1, d)), _const_spec((1, d))],
        out_specs=_tile_spec(tm),
        scratch_shapes=[pltpu.VMEM((SUBLANES + tm, inner), F32),
                        pltpu.VMEM((tm, inner), F32), pltpu.VMEM((tm, inner), F32),
                        pltpu.VMEM((tm, inner), F32),
                        pltpu.VMEM((nh, hd, hd), F32), pltpu.VMEM((nh, 1, hd), F32),
                        pltpu.VMEM((nh, SUBLANES, LANES), F32),
                        pltpu.VMEM((tm, inner), BF16)],
        compiler_params=_params(),
        name="mlstm",
    )(x, mod, w_up.astype(BF16), conv_w, row(conv_b), wqk, wv, wg, bg,
      row(norm_g), row(skip), w_down.astype(BF16), tri, row(pg), row(pb))


def _sb_kernel(x_ref, mod_ref, wqkv_ref, wout_ref, g_ref, b_ref, o_ref,
               q_scr, k_scr, v_scr, z_a, z_b, hl_a, hl_b, pav, later_scr, acc_scr, bias_scr, cum_scr, *, t):
    i = pl.program_id(1)
    d = D_MODEL
    nhp = d // LANES
    nblk = i + 1
    x = x_ref[...]
    h = _modulate(x, mod_ref, 0).astype(BF16)
    lane = lax.broadcasted_iota(jnp.int32, (1, LANES), 1)
    head_masks = (lane < SB_HDIM, lane >= SB_HDIM)
    base = pl.multiple_of(i * t, t)
    q = _dot(h, wqkv_ref[:, 0:d]) * (SB_HDIM ** -0.5 * LOG2_E)
    kk = _dot(h, wqkv_ref[:, d:2 * d])
    vv = _dot(h, wqkv_ref[:, 2 * d:3 * d])
    for hp in range(nhp):
        sl = slice(hp * LANES, (hp + 1) * LANES)
        k_scr[hp, pl.ds(base, t), :] = kk[:, sl].astype(BF16)
        for s in range(2):
            q_scr[hp, s * t:(s + 1) * t, :] = jnp.where(head_masks[s], q[:, sl], 0.0).astype(BF16)
            v_scr[hp, s, pl.ds(base, t), :] = jnp.where(head_masks[s], vv[:, sl], 0.0).astype(BF16)

    row = lax.broadcasted_iota(jnp.int32, (t, t), 0)
    col = lax.broadcasted_iota(jnp.int32, (t, t), 1)
    bias_scr[0] = jnp.zeros((t, t), F32)
    bias_scr[1] = jnp.where(col < row, 0.0, SB_MASKED)
    incl_ones = jnp.where(row >= col, 1.0, 0.0).astype(BF16)
    cum_scr[0:t, :] = incl_ones
    cum_scr[t:2 * t, :] = incl_ones
    pav[...] = jnp.zeros_like(pav)
    later_scr[...] = jnp.zeros_like(later_scr)
    acc_scr[...] = jnp.zeros_like(acc_scr)

    def scores(hp, jj, zbuf, hlbuf):
        start = pl.multiple_of((i - jj) * t, t)
        z = _dot_nt(q_scr[hp], k_scr[hp, pl.ds(start, t), :])
        z = (z.reshape(2, t, t) + bias_scr[jnp.where(jj == 0, 1, 0)]).reshape(2 * t, t)
        zbuf[...] = z
        sp = jnp.maximum(z, 0.0) + jnp.log(1.0 + jnp.exp2(-jnp.abs(z))) * LOG2_E
        hi = sp.astype(BF16)
        hlbuf[:, 0:t] = hi
        hlbuf[:, t:2 * t] = (sp - hi.astype(F32)).astype(BF16)

    def weights(hp, jj, zbuf, hlbuf):
        incl = _dot(hlbuf[...], cum_scr[...])
        later = jnp.where(jj == 0, 0.0, later_scr[...])
        a = jnp.exp2(zbuf[...] - incl - jnp.concatenate([later] * (t // LANES), axis=1)).astype(BF16)
        later_scr[...] = later + jnp.broadcast_to(incl[:, 0:1], (2 * t, LANES))
        start = pl.multiple_of((i - jj) * t, t)
        vst = jnp.concatenate([v_scr[hp, 0, pl.ds(start, t), :], v_scr[hp, 1, pl.ds(start, t), :]], axis=0)
        return _dot(jnp.concatenate([a[0:t], a[t:2 * t]], axis=1), vst)

    def advance(hp, jj):
        last = jj + 1 >= nblk
        return (jnp.where(last, jnp.minimum(hp + 1, nhp - 1), hp), jnp.where(last, 0, jj + 1))

    scores(0, 0, z_a, hl_a)

    def body(_, carry):
        hp_prev, hp0, jj0 = carry
        hp1, jj1 = advance(hp0, jj0)
        hp2, jj2 = advance(hp1, jj1)
        acc_scr[hp_prev] += pav[...]
        scores(hp1, jj1, z_b, hl_b)
        p0 = weights(hp0, jj0, z_a, hl_a)
        scores(hp2, jj2, z_a, hl_a)
        acc_scr[hp0] += p0
        pav[...] = weights(hp1, jj1, z_b, hl_b)
        return hp1, hp2, jj2

    hp_last, _, _ = lax.fori_loop(0, (nhp // 2) * nblk, body, (0, 0, 0))
    acc_scr[hp_last] += pav[...]
    o = jnp.concatenate([acc_scr[hp] for hp in range(nhp)], axis=1).astype(BF16)
    y = _dot(o, wout_ref[...])
    o_ref[...] = _post_norm(x, y, mod_ref, 0, g_ref, b_ref)


def _sb_call(x, mod, w_qkv, w_out, pg, pb):
    bsz, seq, d = x.shape
    t = min(SB_T, seq)
    nhp = d // LANES
    kern = functools.partial(_sb_kernel, t=t)
    row = lambda v: v.reshape(1, -1)
    return pl.pallas_call(
        kern,
        out_shape=jax.ShapeDtypeStruct(x.shape, F32),
        grid=(bsz, seq // t),
        in_specs=[_tile_spec(t), _mod_spec(), _const_spec((d, 3 * d)), _const_spec((d, d)),
                  _const_spec((1, d)), _const_spec((1, d))],
        out_specs=_tile_spec(t),
        scratch_shapes=[pltpu.VMEM((nhp, 2 * t, LANES), BF16),
                        pltpu.VMEM((nhp, seq, LANES), BF16),
                        pltpu.VMEM((nhp, 2, seq, LANES), BF16),
                        pltpu.VMEM((2 * t, t), F32), pltpu.VMEM((2 * t, t), F32),
                        pltpu.VMEM((2 * t, 2 * t), BF16), pltpu.VMEM((2 * t, 2 * t), BF16),
                        pltpu.VMEM((t, LANES), F32),
                        pltpu.VMEM((2 * t, LANES), F32),
                        pltpu.VMEM((nhp, t, LANES), F32),
                        pltpu.VMEM((2, t, t), F32),
                        pltpu.VMEM((2 * t, t), BF16)],
        compiler_params=_params(),
        name="stick_breaking",
    )(x, mod, w_qkv.astype(BF16), w_out.astype(BF16), row(pg), row(pb))


def kernel(x, c, ada_w, ada_b, post_ln_g, post_ln_b, ffn_w_up, ffn_conv_w, ffn_conv_b, ffn_w_down, cc_w_pw1, cc_b_pw1, cc_dw_w, cc_dw_b, cc_ln_g, cc_ln_b, cc_w_pw2, cc_b_pw2, hg_lb_logits, hg_w_in, hg_norm_g, hg_w_out, ml_w_up, ml_conv_w, ml_conv_b, ml_w_q, ml_w_k, ml_w_v, ml_w_gates, ml_b_gates, ml_norm_g, ml_skip, ml_w_down, sb_w_qkv, sb_w_out):
    bsz = x.shape[0]
    mods = _ada_call(c, ada_w, ada_b).reshape(DEPTH, bsz, 6, D_MODEL)
    for i in range(DEPTH):
        kind, j = i % 4, i // 4
        mod = mods[i]
        pg, pb = post_ln_g[i, 0], post_ln_b[i, 0]
        if kind == 0:
            x = _conf_call(x, mod, cc_w_pw1[j], cc_b_pw1[j], cc_dw_w[j], cc_dw_b[j], cc_ln_g[j], cc_ln_b[j],
                           cc_w_pw2[j], cc_b_pw2[j], pg, pb)
        elif kind == 1:
            x = _hgrn_call(x, mod, i, hg_lb_logits, hg_w_in[j], hg_norm_g[j], hg_w_out[j], pg, pb)
        elif kind == 2:
            x = _mlstm_call(x, mod, ml_w_up[j], ml_conv_w[j], ml_conv_b[j], ml_w_q[j], ml_w_k[j], ml_w_v[j],
                            ml_w_gates[j], ml_b_gates[j], ml_norm_g[j], ml_skip[j], ml_w_down[j], pg, pb)
        else:
            x = _sb_call(x, mod, sb_w_qkv[j], sb_w_out[j], pg, pb)
        x = _ffn_call(x, mod, ffn_w_up[i], ffn_conv_w[i], ffn_conv_b[i], ffn_w_down[i],
                      post_ln_g[i, 1], post_ln_b[i, 1])
    return x
```

```python
import functools

import jax
import jax.numpy as jnp
from jax import lax
from jax.experimental import pallas as pl
from jax.experimental.pallas import tpu as pltpu

F32 = jnp.float32
BF16 = jnp.bfloat16

D_MODEL = 1024
DEPTH = 4
D_FF = 2816
FFN_CONV = 3
CONV_WIDTH = 31
HGRN_HEADS = 8
HGRN_DK = 128
HGRN_CHUNK = 32
MLSTM_INNER = 2048
MLSTM_HEADS = 4
MLSTM_HDIM = 512
MLSTM_CONV = 4
MLSTM_QKV_BLOCK = 4
SB_HEADS = 16
SB_HDIM = 64
LOG2_E = 1.4426950408889634
SB_MASKED = -1e30
DN_ALPHA = (2.0 * DEPTH) ** 0.25
LN_EPS = 1e-5
RMS_EPS = 1e-6

VMEM_LIMIT_BYTES = 56 * 1024 * 1024
LANES = 128
SUBLANES = 8
MXU_DIM = 256

FFN_TM = 512
FFN_FC = 256
CONF_TM = 512
CONF_RB = 256
CONF_HALO = 32
HGRN_TM = 256
MLSTM_TM = 256
SB_T = 256


def _sigmoid(x):
    return 1.0 / (1.0 + jnp.exp(-x))


def _silu(x):
    return x * _sigmoid(x)


def _dot(a, b):
    return jnp.dot(a, b, preferred_element_type=F32)


def _dot_nt(a, b):
    return lax.dot_general(a, b, (((1,), (1,)), ((), ())), preferred_element_type=F32)


def _dot_tn(a, b):
    return lax.dot_general(a, b, (((0,), (0,)), ((), ())), preferred_element_type=F32)


def _split3(x):
    hi = x.astype(BF16)
    r1 = x - hi.astype(F32)
    mid = r1.astype(BF16)
    lo = (r1 - mid.astype(F32)).astype(BF16)
    return hi, mid, lo


def _ln_rows(r, g, b, eps):
    mu = jnp.mean(r, axis=-1, keepdims=True)
    d = r - mu
    var = jnp.mean(d * d, axis=-1, keepdims=True)
    return d * lax.rsqrt(var + eps) * g + b


def _modulate(x, mod_ref, which):
    base = 3 * which
    return x * (1.0 + mod_ref[base + 1:base + 2, :]) + mod_ref[base:base + 1, :]


def _post_norm(x, y, mod_ref, which, g_ref, b_ref):
    gate = mod_ref[3 * which + 2:3 * which + 3, :]
    return _ln_rows(DN_ALPHA * x + (1.0 + gate) * y, g_ref[...], b_ref[...], LN_EPS)


def _const_spec(shape):
    nd = len(shape)
    return pl.BlockSpec(shape, lambda *_: (0,) * nd, pipeline_mode=pl.Buffered(1))


def _tile_spec(tm):
    return pl.BlockSpec((None, tm, D_MODEL), lambda b, i: (b, i, 0))


def _mod_spec():
    return pl.BlockSpec((None, 6, D_MODEL), lambda b, i: (b, 0, 0))


def _params():
    return pltpu.CompilerParams(dimension_semantics=("arbitrary", "arbitrary"),
                                vmem_limit_bytes=VMEM_LIMIT_BYTES)


def _ada_kernel(c_ref, w_ref, b_ref, o_ref):
    o_ref[...] = _dot(_silu(c_ref[...]), w_ref[...]) + b_ref[...]


def _ada_call(c, ada_w, ada_b):
    depth, d, n = ada_w.shape
    bsz = c.shape[0]
    tn = 1536
    return pl.pallas_call(
        _ada_kernel,
        out_shape=jax.ShapeDtypeStruct((depth, bsz, n), F32),
        grid=(depth, n // tn),
        in_specs=[pl.BlockSpec((bsz, d), lambda l, j: (0, 0)),
                  pl.BlockSpec((None, d, tn), lambda l, j: (l, 0, j)),
                  pl.BlockSpec((None, 1, tn), lambda l, j: (l, 0, j))],
        out_specs=pl.BlockSpec((None, bsz, tn), lambda l, j: (l, 0, j)),
        compiler_params=_params(),
        name="ada_mod",
    )(c, ada_w, ada_b.reshape(depth, 1, n))


def _ffn_kernel(x_ref, mod_ref, wup_ref, cw_ref, cb_ref, wdn_ref, g_ref, b_ref, o_ref,
                hbuf, ubuf_a, ubuf_b, carry, acc_ref, *, tm, fc, nf):
    i = pl.program_id(1)

    @pl.when(i == 0)
    def _():
        carry[...] = jnp.zeros_like(carry)

    x = x_ref[...]
    hbuf[...] = _modulate(x, mod_ref, 1).astype(BF16)
    acc_ref[...] = jnp.zeros_like(acc_ref)

    ng = 2 * fc // LANES

    def up(j, ubuf):
        u = _dot(hbuf[...], wup_ref[j])
        ubuf[:, 0:SUBLANES, :] = carry[j]
        for g in range(ng):
            ubuf[g, SUBLANES:SUBLANES + tm, :] = u[:, g * LANES:(g + 1) * LANES]

    def down(j, ubuf):
        carry[j] = ubuf[:, tm:tm + SUBLANES, :]
        ys = []
        for g in range(ng):
            lanes = slice(g * LANES, (g + 1) * LANES)
            y = cb_ref[j, :, lanes]
            for k in range(FFN_CONV):
                off = SUBLANES - (FFN_CONV - 1) + k
                y = y + cw_ref[j, k:k + 1, lanes] * ubuf[g, off:off + tm, :]
            ys.append(y)
        gate = jnp.concatenate(ys[:ng // 2], axis=1)
        val = jnp.concatenate(ys[ng // 2:], axis=1)
        acc_ref[...] += _dot((_silu(gate) * val).astype(BF16), wdn_ref[j])

    up(0, ubuf_a)

    def body(p, _):
        j = 2 * p
        up(j + 1, ubuf_b)
        down(j, ubuf_a)
        up(j + 2, ubuf_a)
        down(j + 1, ubuf_b)
        return 0

    lax.fori_loop(0, (nf - 1) // 2, body, 0)
    if nf % 2 == 0:
        up(nf - 1, ubuf_b)
        down(nf - 2, ubuf_a)
        down(nf - 1, ubuf_b)
    else:
        down(nf - 1, ubuf_a)
    o_ref[...] = _post_norm(x, acc_ref[...], mod_ref, 1, g_ref, b_ref)


def _ffn_call(x, mod, w_up, conv_w, conv_b, w_down, ln_g, ln_b):
    bsz, seq, d = x.shape
    f = w_down.shape[0]
    tm, fc = min(FFN_TM, seq), FFN_FC
    nf = f // fc
    wup = w_up.astype(BF16).reshape(d, 2, nf, fc).transpose(2, 0, 1, 3).reshape(nf, d, 2 * fc)
    cw = conv_w.reshape(FFN_CONV, 2, nf, fc).transpose(2, 0, 1, 3).reshape(nf, FFN_CONV, 2 * fc)
    cb = conv_b.reshape(2, nf, fc).transpose(1, 0, 2).reshape(nf, 1, 2 * fc)
    wdn = w_down.astype(BF16).reshape(nf, fc, d)
    kern = functools.partial(_ffn_kernel, tm=tm, fc=fc, nf=nf)
    return pl.pallas_call(
        kern,
        out_shape=jax.ShapeDtypeStruct(x.shape, F32),
        grid=(bsz, seq // tm),
        in_specs=[_tile_spec(tm), _mod_spec(),
                  _const_spec(wup.shape), _const_spec(cw.shape), _const_spec(cb.shape),
                  _const_spec(wdn.shape), _const_spec((1, d)), _const_spec((1, d))],
        out_specs=_tile_spec(tm),
        scratch_shapes=[pltpu.VMEM((tm, d), BF16),
                        pltpu.VMEM((2 * fc // LANES, SUBLANES + tm, LANES), F32),
                        pltpu.VMEM((2 * fc // LANES, SUBLANES + tm, LANES), F32),
                        pltpu.VMEM((nf, 2 * fc // LANES, SUBLANES, LANES), F32),
                        pltpu.VMEM((tm, d), F32)],
        compiler_params=_params(),
        name="conv_ffn",
    )(x, mod, wup, cw, cb, wdn, ln_g.reshape(1, d), ln_b.reshape(1, d))


def _conf_kernel(x_ref, mod_ref, w1_ref, b1_ref, dw_ref, dwb_ref, lng_ref, lnb_ref, w2_ref, b2_ref,
                 g_ref, b_ref, o_ref, cbuf, ybuf, *, tm):
    i = pl.program_id(1)
    d = D_MODEL
    halo = CONF_HALO
    ngrp = d // LANES

    @pl.when(i == 0)
    def _():
        cbuf[:, 0:halo, :] = jnp.zeros((ngrp, halo, LANES), F32)

    x = x_ref[...]
    h = _modulate(x, mod_ref, 0).astype(BF16)
    per = MXU_DIM // LANES
    for c in range(d // MXU_DIM):
        lo, hi = c * MXU_DIM, (c + 1) * MXU_DIM
        a = _dot(h, w1_ref[:, lo:hi]) + b1_ref[:, lo:hi]
        g = _dot(h, w1_ref[:, d + lo:d + hi]) + b1_ref[:, d + lo:d + hi]
        u = a * _sigmoid(g)
        for s in range(per):
            cbuf[c * per + s, halo:halo + tm, :] = u[:, s * LANES:(s + 1) * LANES]

    first = halo - (CONV_WIDTH - 1)

    def conv_body(c, _):
        for r0 in range(0, tm, CONF_RB):
            acc = jnp.broadcast_to(dwb_ref[c], (CONF_RB, LANES))
            for k in range(CONV_WIDTH):
                acc = acc + dw_ref[c, k:k + 1, :] * cbuf[c, r0 + first + k:r0 + first + k + CONF_RB, :]
            ybuf[c, r0:r0 + CONF_RB, :] = acc
        cbuf[c, 0:halo, :] = cbuf[c, tm:tm + halo, :]
        return 0

    lax.fori_loop(0, ngrp, conv_body, 0)

    y = jnp.concatenate([ybuf[c] for c in range(ngrp)], axis=1)
    z = _silu(_ln_rows(y, lng_ref[...], lnb_ref[...], LN_EPS))
    out = _dot(z.astype(BF16), w2_ref[...]) + b2_ref[...]
    o_ref[...] = _post_norm(x, out, mod_ref, 0, g_ref, b_ref)


def _conf_call(x, mod, w_pw1, b_pw1, dw_w, dw_b, ln_g, ln_b, w_pw2, b_pw2, pg, pb):
    bsz, seq, d = x.shape
    tm = min(CONF_TM, seq)
    ngrp = d // LANES
    dw = dw_w.reshape(CONV_WIDTH, ngrp, LANES).transpose(1, 0, 2)
    dwb = dw_b.reshape(ngrp, 1, LANES)
    kern = functools.partial(_conf_kernel, tm=tm)
    row = lambda v: v.reshape(1, -1)
    return pl.pallas_call(
        kern,
        out_shape=jax.ShapeDtypeStruct(x.shape, F32),
        grid=(bsz, seq // tm),
        in_specs=[_tile_spec(tm), _mod_spec(),
                  _const_spec((d, 2 * d)), _const_spec((1, 2 * d)),
                  _const_spec(dw.shape), _const_spec(dwb.shape),
                  _const_spec((1, d)), _const_spec((1, d)),
                  _const_spec((d, d)), _const_spec((1, d)),
                  _const_spec((1, d)), _const_spec((1, d))],
        out_specs=_tile_spec(tm),
        scratch_shapes=[pltpu.VMEM((ngrp, CONF_HALO + tm, LANES), F32),
                        pltpu.VMEM((ngrp, tm, LANES), F32)],
        compiler_params=_params(),
        name="conformer_conv",
    )(x, mod, w_pw1.astype(BF16), row(b_pw1), dw, dwb, row(ln_g), row(ln_b),
      w_pw2.astype(BF16), row(b_pw2), row(pg), row(pb))


def _hgrn_kernel(x_ref, mod_ref, lb_ref, win_ref, ng_ref, wout_ref, tri_ref, g_ref, b_ref, o_ref,
                 st_ref, obuf, *, tm, layer):
    i = pl.program_id(1)
    d = D_MODEL
    ch = HGRN_CHUNK
    nch = tm // ch

    @pl.when(i == 0)
    def _():
        st_ref[...] = jnp.zeros_like(st_ref)

    x = x_ref[...]
    h = _modulate(x, mod_ref, 0).astype(BF16)
    q = _silu(_dot(h, win_ref[:, 0:d]))
    logits = lb_ref[...]
    e = jnp.exp(logits - jnp.max(logits, axis=0, keepdims=True))
    sm = e / jnp.sum(e, axis=0, keepdims=True)
    lb = jnp.sum(sm[1:layer + 1, :], axis=0, keepdims=True) if layer > 0 else jnp.zeros((1, d), F32)
    f = lb + (1.0 - lb) * _sigmoid(_dot(h, win_ref[:, d:2 * d]))
    k = 1.0 - f
    lf = jnp.log(f)
    tri = tri_ref[...]
    bcum = sum(_dot(tri, part) for part in _split3(lf))
    b3 = bcum.reshape(nch, ch, d)
    b_mid = b3[:, ch // 2 - 1:ch // 2, :]
    b_last = b3[:, ch - 1:ch, :]
    q3 = q.reshape(nch, ch, d)
    k3 = k.reshape(nch, ch, d)
    qs = (q3 * jnp.exp(b3 - b_mid)).reshape(tm, d).astype(BF16)
    ks = (k3 * jnp.exp(b_mid - b3)).reshape(tm, d).astype(BF16)
    qd = (q3 * jnp.exp(b3)).reshape(tm, d).astype(BF16)
    kd = (k3 * jnp.exp(b_last - b3)).reshape(tm, d).astype(BF16)
    dec = jnp.exp(b_last)
    vb = _dot(h, win_ref[:, 2 * d:3 * d]).astype(BF16)

    causal = tri.astype(F32) > 0.5

    for hh in range(HGRN_HEADS):
        sl = slice(hh * HGRN_DK, (hh + 1) * HGRN_DK)
        sc = _dot_nt(qs[:, sl], ks[:, sl])
        o_intra = _dot(jnp.where(causal, sc, 0.0).astype(BF16), vb[:, sl])
        st = st_ref[hh]
        parts = []
        for c in range(nch):
            rows = slice(c * ch, (c + 1) * ch)
            parts.append(_dot_nt(qd[rows, sl], st.astype(BF16)))
            st = dec[c, :, sl] * st + _dot_tn(vb[rows, sl], kd[rows, sl])
        st_ref[hh] = st
        o = o_intra + jnp.concatenate(parts, axis=0)
        o = o * lax.rsqrt(jnp.mean(o * o, axis=-1, keepdims=True) + RMS_EPS) * ng_ref[...]
        obuf[:, sl] = o
    gate = _silu(_dot(h, win_ref[:, 3 * d:4 * d]))
    y = _dot((obuf[...] * gate).astype(BF16), wout_ref[...])
    o_ref[...] = _post_norm(x, y, mod_ref, 0, g_ref, b_ref)


def _hgrn_call(x, mod, layer, lb_logits, w_in, norm_g, w_out, pg, pb):
    bsz, seq, d = x.shape
    tm = min(HGRN_TM, seq)
    r = jnp.arange(tm)
    tri = ((r[:, None] // HGRN_CHUNK == r[None, :] // HGRN_CHUNK) & (r[None, :] <= r[:, None])).astype(BF16)
    kern = functools.partial(_hgrn_kernel, tm=tm, layer=layer)
    row = lambda v: v.reshape(1, -1)
    return pl.pallas_call(
        kern,
        out_shape=jax.ShapeDtypeStruct(x.shape, F32),
        grid=(bsz, seq // tm),
        in_specs=[_tile_spec(tm), _mod_spec(), _const_spec(lb_logits.shape),
                  _const_spec((d, 4 * d)), _const_spec((1, HGRN_DK)), _const_spec((d, d)),
                  _const_spec((tm, tm)), _const_spec((1, d)), _const_spec((1, d))],
        out_specs=_tile_spec(tm),
        scratch_shapes=[pltpu.VMEM((HGRN_HEADS, HGRN_DK, HGRN_DK), F32),
                        pltpu.VMEM((tm, d), F32)],
        compiler_params=_params(),
        name="hgrn2",
    )(x, mod, lb_logits.astype(F32), w_in.astype(BF16), row(norm_g), w_out.astype(BF16), tri, row(pg), row(pb))


def _mlstm_kernel(x_ref, mod_ref, wup_ref, cw_ref, cb_ref, wqk_ref, wv_ref, wg_ref, bg_ref,
                  ng_ref, sk_ref, wdn_ref, tri_ref, g_ref, b_ref, o_ref,
                  xbuf, qbuf, kbuf, vbuf, c_ref, n_ref, m_ref, hbuf, *, tm):
    i = pl.program_id(1)
    d = D_MODEL
    inner = MLSTM_INNER
    hd = MLSTM_HDIM
    nh = MLSTM_HEADS
    blk = MXU_DIM
    pad = SUBLANES

    @pl.when(i == 0)
    def _():
        c_ref[...] = jnp.zeros_like(c_ref)
        n_ref[...] = jnp.zeros_like(n_ref)
        m_ref[...] = jnp.zeros_like(m_ref)
        xbuf[:, 0:pad, :] = jnp.zeros((inner // LANES, pad, LANES), F32)

    x = x_ref[...]
    h = _modulate(x, mod_ref, 0).astype(BF16)
    xm = _dot(h, wup_ref[:, 0:inner])
    convs = []
    for g in range(inner // LANES):
        lanes = slice(g * LANES, (g + 1) * LANES)
        xbuf[g, pad:pad + tm, :] = xm[:, lanes]
        cv = cw_ref[MLSTM_CONV - 1:MLSTM_CONV, lanes] * xm[:, lanes] + cb_ref[:, lanes]
        for kk in range(MLSTM_CONV - 1):
            off = pad - (MLSTM_CONV - 1) + kk
            cv = cv + cw_ref[kk:kk + 1, lanes] * xbuf[g, off:off + tm, :]
        convs.append(cv)
        xbuf[g, 0:pad, :] = xm[tm - pad:tm, lanes]
    xc = _silu(jnp.concatenate(convs, axis=1))
    xcb = xc.astype(BF16)
    xmb = xm.astype(BF16)
    for j in range(inner // blk):
        cols = slice(j * blk, (j + 1) * blk)
        qk = _dot(xcb[:, cols], wqk_ref[j])
        qbuf[:, cols] = qk[:, 0:blk]
        kbuf[:, cols] = qk[:, blk:2 * blk]
        vbuf[:, cols] = _dot(xmb[:, cols], wv_ref[j])
    qf = qbuf[...]
    kf = kbuf[...]
    vf = vbuf[...]
    qb = qf.astype(BF16)
    vb = vf.astype(BF16)
    gates = (_dot(qb, wg_ref[0:inner, :]) + _dot(kf.astype(BF16), wg_ref[inner:2 * inner, :])
             + _dot(vb, wg_ref[2 * inner:3 * inner, :]) + bg_ref[...])
    log_f = jnp.minimum(gates, 0.0) - jnp.log(1.0 + jnp.exp(-jnp.abs(gates)))
    tri = tri_ref[...]
    bc = sum(_dot(tri, part) for part in _split3(log_f))
    gates_t = gates.T
    bc_t = bc.T
    row = lax.broadcasted_iota(jnp.int32, (tm, tm), 0)
    col = lax.broadcasted_iota(jnp.int32, (tm, tm), 1)
    causal = col <= row
    kscaled = kf * (hd ** -0.5)

    for hh in range(nh):
        sl = slice(hh * hd, (hh + 1) * hd)
        bt_col = bc[:, nh + hh:nh + hh + 1]
        it_col = gates[:, hh:hh + 1]
        bt_row = bc_t[nh + hh:nh + hh + 1, :]
        it_row = gates_t[hh:hh + 1, :]
        m_prev = m_ref[hh][0:1, 0:1]
        log_w = jnp.where(causal, bt_col - bt_row + it_row, -jnp.inf)
        log_inter = bt_col + m_prev
        m_t = jnp.maximum(jnp.max(log_w, axis=-1, keepdims=True), log_inter)
        kh = kscaled[:, sl]
        khb = kh.astype(BF16)
        qh = qf[:, sl]
        qhb = qb[:, sl]
        vhb = vb[:, sl]
        s_qk = _dot_nt(qhb, khb) * jnp.exp(log_w - m_t)
        w_inter = jnp.exp(log_inter - m_t)
        cm = c_ref[hh]
        nvec = n_ref[hh]
        num = _dot(s_qk.astype(BF16), vhb) + w_inter * _dot(qhb, cm.astype(BF16))
        den = jnp.sum(s_qk, axis=-1, keepdims=True) + w_inter * jnp.sum(qh * nvec, axis=-1, keepdims=True)
        h_out = num / jnp.maximum(jnp.abs(den), jnp.exp(-m_t))
        b_last = bt_col[tm - 1:tm, :]
        log_ws = b_last - bt_col + it_col
        m_new = jnp.maximum(b_last + m_prev, jnp.max(log_ws, axis=0, keepdims=True))
        ws = jnp.exp(log_ws - m_new)
        decay = jnp.exp(b_last + m_prev - m_new)
        kw = kh * ws
        c_ref[hh] = decay * cm + _dot_tn(kw.astype(BF16), vhb)
        n_ref[hh] = decay * nvec + jnp.sum(kw, axis=0, keepdims=True)
        m_ref[hh] = jnp.broadcast_to(m_new, (SUBLANES, LANES))
        mu = jnp.mean(h_out, axis=-1, keepdims=True)
        dlt = h_out - mu
        var = jnp.mean(dlt * dlt, axis=-1, keepdims=True)
        hn = dlt * lax.rsqrt(var + LN_EPS)
        z = _dot(h, wup_ref[:, inner + hh * hd:inner + (hh + 1) * hd])
        hbuf[:, sl] = ((hn * ng_ref[:, sl] + sk_ref[:, sl] * xc[:, sl]) * _silu(z)).astype(BF16)
    y = _dot(hbuf[...], wdn_ref[...])
    o_ref[...] = _post_norm(x, y, mod_ref, 0, g_ref, b_ref)


def _block_diag_tiles(w, tile):
    g, blk, _ = w.shape
    per = tile // blk
    w4 = w.reshape(g // per, per, blk, blk)
    eye = jnp.eye(per, dtype=w.dtype)
    t = jnp.einsum('npio,pq->npiqo', w4, eye)
    return t.reshape(g // per, tile, tile)


def _mlstm_call(x, mod, w_up, conv_w, conv_b, w_q, w_k, w_v, w_gates, b_gates, norm_g, skip, w_down, pg, pb):
    bsz, seq, d = x.shape
    tm = min(MLSTM_TM, seq)
    inner, nh, hd = MLSTM_INNER, MLSTM_HEADS, MLSTM_HDIM
    wq_t = _block_diag_tiles(w_q, MXU_DIM)
    wk_t = _block_diag_tiles(w_k, MXU_DIM)
    wqk = jnp.concatenate([wq_t, wk_t], axis=-1).astype(BF16)
    wv = _block_diag_tiles(w_v, MXU_DIM).astype(BF16)
    wg = jnp.pad(w_gates, ((0, 0), (0, LANES - 2 * nh))).astype(BF16)
    bg = jnp.pad(b_gates, (0, LANES - 2 * nh)).reshape(1, LANES)
    r = jnp.arange(tm)
    tri = (r[None, :] <= r[:, None]).astype(BF16)
    kern = functools.partial(_mlstm_kernel, tm=tm)
    row = lambda v: v.reshape(1, -1)
    return pl.pallas_call(
        kern,
        out_shape=jax.ShapeDtypeStruct(x.shape, F32),
        grid=(bsz, seq // tm),
        in_specs=[_tile_spec(tm), _mod_spec(),
                  _const_spec((d, 2 * inner)), _const_spec((MLSTM_CONV, inner)), _const_spec((1, inner)),
                  _const_spec(wqk.shape), _const_spec(wv.shape),
                  _const_spec(wg.shape), _const_spec((1, LANES)),
                  _const_spec((1, inner)), _const_spec((1, inner)), _const_spec((inner, d)),
                  _const_spec((tm, tm)), _const_spec((1, d)), _const_spec((1, d))],
        out_specs=_tile_spec(tm),
        scratch_shapes=[pltpu.VMEM((inner // LANES, SUBLANES + tm, LANES), F32),
                        pltpu.VMEM((tm, inner), F32), pltpu.VMEM((tm, inner), F32),
                        pltpu.VMEM((tm, inner), F32),
                        pltpu.VMEM((nh, hd, hd), F32), pltpu.VMEM((nh, 1, hd), F32),
                        pltpu.VMEM((nh, SUBLANES, LANES), F32),
                        pltpu.VMEM((tm, inner), BF16)],
        compiler_params=_params(),
        name="mlstm",
    )(x, mod, w_up.astype(BF16), conv_w, row(conv_b), wqk, wv, wg, bg,
      row(norm_g), row(skip), w_down.astype(BF16), tri, row(pg), row(pb))


def _sb_kernel(x_ref, mod_ref, wqkv_ref, wout_ref, g_ref, b_ref, o_ref,
               q_scr, k_scr, v_scr, z_a, z_b, w_a, w_b, r_a, r_b, pav, later_scr, acc_scr, bias_scr, cum_scr,
               *, t):
    i = pl.program_id(1)
    d = D_MODEL
    nhp = d // LANES
    nblk = i + 1
    x = x_ref[...]
    h = _modulate(x, mod_ref, 0).astype(BF16)
    lane = lax.broadcasted_iota(jnp.int32, (1, LANES), 1)
    head_masks = (lane < SB_HDIM, lane >= SB_HDIM)
    base = pl.multiple_of(i * t, t)
    q = _dot(h, wqkv_ref[:, 0:d]) * (SB_HDIM ** -0.5 * LOG2_E)
    kk = _dot(h, wqkv_ref[:, d:2 * d])
    vv = _dot(h, wqkv_ref[:, 2 * d:3 * d])
    for hp in range(nhp):
        sl = slice(hp * LANES, (hp + 1) * LANES)
        k_scr[hp, pl.ds(base, t), :] = kk[:, sl].astype(BF16)
        for s in range(2):
            q_scr[hp, s * t:(s + 1) * t, :] = jnp.where(head_masks[s], q[:, sl], 0.0).astype(BF16)
            v_scr[hp, s, pl.ds(base, t), :] = jnp.where(head_masks[s], vv[:, sl], 0.0).astype(BF16)

    row = lax.broadcasted_iota(jnp.int32, (t, t), 0)
    col = lax.broadcasted_iota(jnp.int32, (t, t), 1)
    bias_scr[0] = jnp.zeros((t, t), F32)
    bias_scr[1] = jnp.where(col < row, 0.0, SB_MASKED)
    incl_ones = jnp.where(row >= col, 1.0, 0.0).astype(BF16)
    cum_scr[0:t, :] = incl_ones
    cum_scr[t:2 * t, :] = incl_ones
    pav[...] = jnp.zeros_like(pav)
    later_scr[...] = jnp.zeros_like(later_scr)
    acc_scr[...] = jnp.zeros_like(acc_scr)

    def scores(item, zbuf):
        hp, jj = item
        start = pl.multiple_of((i - jj) * t, t)
        z = _dot_nt(q_scr[hp], k_scr[hp, pl.ds(start, t), :])
        zbuf[...] = (z.reshape(2, t, t) + bias_scr[jnp.where(jj == 0, 1, 0)]).reshape(2 * t, t)

    def cumulate(zbuf, wbuf, rbuf):
        z = zbuf[...]
        sp = jnp.maximum(z, 0.0) + jnp.log(1.0 + jnp.exp2(-jnp.abs(z))) * LOG2_E
        hi = sp.astype(BF16)
        lo = (sp - hi.astype(F32)).astype(BF16)
        incl = _dot(jnp.concatenate([hi, lo], axis=1), cum_scr[...])
        wbuf[...] = z - incl
        rbuf[...] = jnp.broadcast_to(incl[:, 0:1], (2 * t, LANES))

    def weights(item, wbuf, rbuf):
        hp, jj = item
        later = jnp.where(jj == 0, 0.0, later_scr[...])
        a = jnp.exp2(wbuf[...] - jnp.concatenate([later] * (t // LANES), axis=1)).astype(BF16)
        later_scr[...] = later + rbuf[...]
        start = pl.multiple_of((i - jj) * t, t)
        vst = jnp.concatenate([v_scr[hp, 0, pl.ds(start, t), :], v_scr[hp, 1, pl.ds(start, t), :]], axis=0)
        return _dot(jnp.concatenate([a[0:t], a[t:2 * t]], axis=1), vst)

    def advance(item):
        hp, jj = item
        last = jj + 1 >= nblk
        return (jnp.where(last, jnp.minimum(hp + 1, nhp - 1), hp), jnp.where(last, 0, jj + 1))

    first = (jnp.int32(0), jnp.int32(0))
    second = advance(first)
    scores(first, z_a)
    scores(second, z_b)
    cumulate(z_a, w_a, r_a)

    def body(_, carry):
        hp_prev, it0 = carry[0], carry[1:]
        it1 = advance(it0)
        it2 = advance(it1)
        it3 = advance(it2)
        acc_scr[hp_prev] += pav[...]
        p0 = weights(it0, w_a, r_a)
        cumulate(z_b, w_b, r_b)
        scores(it2, z_a)
        acc_scr[it0[0]] += p0
        pav[...] = weights(it1, w_b, r_b)
        cumulate(z_a, w_a, r_a)
        scores(it3, z_b)
        return (it1[0],) + it2

    final = lax.fori_loop(0, (nhp // 2) * nblk, body, (jnp.int32(0),) + first)
    acc_scr[final[0]] += pav[...]
    o = jnp.concatenate([acc_scr[hp] for hp in range(nhp)], axis=1).astype(BF16)
    y = _dot(o, wout_ref[...])
    o_ref[...] = _post_norm(x, y, mod_ref, 0, g_ref, b_ref)


def _sb_call(x, mod, w_qkv, w_out, pg, pb):
    bsz, seq, d = x.shape
    t = min(SB_T, seq)
    nhp = d // LANES
    kern = functools.partial(_sb_kernel, t=t)
    row = lambda v: v.reshape(1, -1)
    return pl.pallas_call(
        kern,
        out_shape=jax.ShapeDtypeStruct(x.shape, F32),
        grid=(bsz, seq // t),
        in_specs=[_tile_spec(t), _mod_spec(), _const_spec((d, 3 * d)), _const_spec((d, d)),
                  _const_spec((1, d)), _const_spec((1, d))],
        out_specs=_tile_spec(t),
        scratch_shapes=[pltpu.VMEM((nhp, 2 * t, LANES), BF16),
                        pltpu.VMEM((nhp, seq, LANES), BF16),
                        pltpu.VMEM((nhp, 2, seq, LANES), BF16),
                        pltpu.VMEM((2 * t, t), F32), pltpu.VMEM((2 * t, t), F32),
                        pltpu.VMEM((2 * t, t), F32), pltpu.VMEM((2 * t, t), F32),
                        pltpu.VMEM((2 * t, LANES), F32), pltpu.VMEM((2 * t, LANES), F32),
                        pltpu.VMEM((t, LANES), F32),
                        pltpu.VMEM((2 * t, LANES), F32),
                        pltpu.VMEM((nhp, t, LANES), F32),
                        pltpu.VMEM((2, t, t), F32),
                        pltpu.VMEM((2 * t, t), BF16)],
        compiler_params=_params(),
        name="stick_breaking",
    )(x, mod, w_qkv.astype(BF16), w_out.astype(BF16), row(pg), row(pb))


def kernel(x, c, ada_w, ada_b, post_ln_g, post_ln_b, ffn_w_up, ffn_conv_w, ffn_conv_b, ffn_w_down, cc_w_pw1, cc_b_pw1, cc_dw_w, cc_dw_b, cc_ln_g, cc_ln_b, cc_w_pw2, cc_b_pw2, hg_lb_logits, hg_w_in, hg_norm_g, hg_w_out, ml_w_up, ml_conv_w, ml_conv_b, ml_w_q, ml_w_k, ml_w_v, ml_w_gates, ml_b_gates, ml_norm_g, ml_skip, ml_w_down, sb_w_qkv, sb_w_out):
    bsz = x.shape[0]
    mods = _ada_call(c, ada_w, ada_b).reshape(DEPTH, bsz, 6, D_MODEL)
    for i in range(DEPTH):
        kind, j = i % 4, i // 4
        mod = mods[i]
        pg, pb = post_ln_g[i, 0], post_ln_b[i, 0]
        if kind == 0:
            x = _conf_call(x, mod, cc_w_pw1[j], cc_b_pw1[j], cc_dw_w[j], cc_dw_b[j], cc_ln_g[j], cc_ln_b[j],
                           cc_w_pw2[j], cc_b_pw2[j], pg, pb)
        elif kind == 1:
            x = _hgrn_call(x, mod, i, hg_lb_logits, hg_w_in[j], hg_norm_g[j], hg_w_out[j], pg, pb)
        elif kind == 2:
            x = _mlstm_call(x, mod, ml_w_up[j], ml_conv_w[j], ml_conv_b[j], ml_w_q[j], ml_w_k[j], ml_w_v[j],
                            ml_w_gates[j], ml_b_gates[j], ml_norm_g[j], ml_skip[j], ml_w_down[j], pg, pb)
        else:
            x = _sb_call(x, mod, sb_w_qkv[j], sb_w_out[j], pg, pb)
        x = _ffn_call(x, mod, ffn_w_up[i], ffn_conv_w[i], ffn_conv_b[i], ffn_w_down[i],
                      post_ln_g[i, 1], post_ln_b[i, 1])
    return x
```

```python
import functools

import jax
import jax.numpy as jnp
from jax import lax
from jax.experimental import pallas as pl
from jax.experimental.pallas import tpu as pltpu

F32 = jnp.float32
BF16 = jnp.bfloat16

D_MODEL = 1024
DEPTH = 4
D_FF = 2816
FFN_CONV = 3
CONV_WIDTH = 31
HGRN_HEADS = 8
HGRN_DK = 128
HGRN_CHUNK = 32
MLSTM_INNER = 2048
MLSTM_HEADS = 4
MLSTM_HDIM = 512
MLSTM_CONV = 4
MLSTM_QKV_BLOCK = 4
SB_HEADS = 16
SB_HDIM = 64
LOG2_E = 1.4426950408889634
SB_MASKED = -1e30
DN_ALPHA = (2.0 * DEPTH) ** 0.25
LN_EPS = 1e-5
RMS_EPS = 1e-6

VMEM_LIMIT_BYTES = 56 * 1024 * 1024
LANES = 128
SUBLANES = 8
MXU_DIM = 256

FFN_TM = 512
FFN_FC = 256
CONF_TM = 512
CONF_RB = 256
CONF_HALO = 32
HGRN_TM = 256
MLSTM_TM = 256
SB_T = 256


def _sigmoid(x):
    return 1.0 / (1.0 + jnp.exp(-x))


def _silu(x):
    return x * _sigmoid(x)


def _dot(a, b):
    return jnp.dot(a, b, preferred_element_type=F32)


def _dot_nt(a, b):
    return lax.dot_general(a, b, (((1,), (1,)), ((), ())), preferred_element_type=F32)


def _dot_tn(a, b):
    return lax.dot_general(a, b, (((0,), (0,)), ((), ())), preferred_element_type=F32)


def _split3(x):
    hi = x.astype(BF16)
    r1 = x - hi.astype(F32)
    mid = r1.astype(BF16)
    lo = (r1 - mid.astype(F32)).astype(BF16)
    return hi, mid, lo


def _ln_rows(r, g, b, eps):
    mu = jnp.mean(r, axis=-1, keepdims=True)
    d = r - mu
    var = jnp.mean(d * d, axis=-1, keepdims=True)
    return d * lax.rsqrt(var + eps) * g + b


def _modulate(x, mod_ref, which):
    base = 3 * which
    return x * (1.0 + mod_ref[base + 1:base + 2, :]) + mod_ref[base:base + 1, :]


def _post_norm(x, y, mod_ref, which, g_ref, b_ref):
    gate = mod_ref[3 * which + 2:3 * which + 3, :]
    return _ln_rows(DN_ALPHA * x + (1.0 + gate) * y, g_ref[...], b_ref[...], LN_EPS)


def _const_spec(shape):
    nd = len(shape)
    return pl.BlockSpec(shape, lambda *_: (0,) * nd, pipeline_mode=pl.Buffered(1))


def _tile_spec(tm):
    return pl.BlockSpec((None, tm, D_MODEL), lambda b, i: (b, i, 0))


def _mod_spec():
    return pl.BlockSpec((None, 6, D_MODEL), lambda b, i: (b, 0, 0))


def _params():
    return pltpu.CompilerParams(dimension_semantics=("arbitrary", "arbitrary"),
                                vmem_limit_bytes=VMEM_LIMIT_BYTES)


def _ada_kernel(c_ref, w_ref, b_ref, o_ref):
    o_ref[...] = _dot(_silu(c_ref[...]), w_ref[...]) + b_ref[...]


def _ada_call(c, ada_w, ada_b):
    depth, d, n = ada_w.shape
    bsz = c.shape[0]
    tn = 1536
    return pl.pallas_call(
        _ada_kernel,
        out_shape=jax.ShapeDtypeStruct((depth, bsz, n), F32),
        grid=(depth, n // tn),
        in_specs=[pl.BlockSpec((bsz, d), lambda l, j: (0, 0)),
                  pl.BlockSpec((None, d, tn), lambda l, j: (l, 0, j)),
                  pl.BlockSpec((None, 1, tn), lambda l, j: (l, 0, j))],
        out_specs=pl.BlockSpec((None, bsz, tn), lambda l, j: (l, 0, j)),
        compiler_params=_params(),
        name="ada_mod",
    )(c, ada_w, ada_b.reshape(depth, 1, n))


def _ffn_kernel(x_ref, mod_ref, wup_ref, cw_ref, cb_ref, wdn_ref, g_ref, b_ref, o_ref,
                hbuf, ubuf_a, ubuf_b, carry, acc_ref, *, tm, fc, nf):
    i = pl.program_id(1)

    @pl.when(i == 0)
    def _():
        carry[...] = jnp.zeros_like(carry)

    x = x_ref[...]
    hbuf[...] = _modulate(x, mod_ref, 1).astype(BF16)
    acc_ref[...] = jnp.zeros_like(acc_ref)

    ng = 2 * fc // LANES

    f = nf * fc
    half = ng // 2

    def cols(j, g):
        base = (0 if g < half else f) + j * fc + (g % half) * LANES
        return slice(base, base + LANES)

    def up(j, ubuf):
        ubuf[:, 0:SUBLANES, :] = carry[j]
        for part in range(2):
            u = _dot(hbuf[...], wup_ref[:, part * f + j * fc:part * f + (j + 1) * fc])
            for s in range(half):
                ubuf[part * half + s, SUBLANES:SUBLANES + tm, :] = u[:, s * LANES:(s + 1) * LANES]

    def down(j, ubuf):
        carry[j] = ubuf[:, tm:tm + SUBLANES, :]
        ys = []
        for g in range(ng):
            y = cb_ref[:, cols(j, g)]
            for k in range(FFN_CONV):
                off = SUBLANES - (FFN_CONV - 1) + k
                y = y + cw_ref[k:k + 1, cols(j, g)] * ubuf[g, off:off + tm, :]
            ys.append(y)
        gate = jnp.concatenate(ys[:half], axis=1)
        val = jnp.concatenate(ys[half:], axis=1)
        acc_ref[...] += _dot((_silu(gate) * val).astype(BF16), wdn_ref[j * fc:(j + 1) * fc, :])

    bufs = (ubuf_a, ubuf_b)
    up(0, bufs[0])
    for j in range(nf):
        if j + 1 < nf:
            up(j + 1, bufs[(j + 1) % 2])
        down(j, bufs[j % 2])
    o_ref[...] = _post_norm(x, acc_ref[...], mod_ref, 1, g_ref, b_ref)


def _layer_spec(shape, layer):
    rest = tuple(shape[1:])
    return pl.BlockSpec((None,) + rest, lambda *_: (layer,) + (0,) * len(rest), pipeline_mode=pl.Buffered(1))


def _ffn_call(x, mod, layer, w_up, conv_w, conv_b, w_down, ln_g, ln_b):
    bsz, seq, d = x.shape
    f = w_down.shape[1]
    tm, fc = min(FFN_TM, seq), FFN_FC
    nf = f // fc
    kern = functools.partial(_ffn_kernel, tm=tm, fc=fc, nf=nf)
    return pl.pallas_call(
        kern,
        out_shape=jax.ShapeDtypeStruct(x.shape, F32),
        grid=(bsz, seq // tm),
        in_specs=[_tile_spec(tm), _mod_spec(),
                  _layer_spec(w_up.shape, layer), _layer_spec(conv_w.shape, layer),
                  _layer_spec(conv_b.shape, layer), _layer_spec(w_down.shape, layer),
                  _const_spec((1, d)), _const_spec((1, d))],
        out_specs=_tile_spec(tm),
        scratch_shapes=[pltpu.VMEM((tm, d), BF16),
                        pltpu.VMEM((2 * fc // LANES, SUBLANES + tm, LANES), F32),
                        pltpu.VMEM((2 * fc // LANES, SUBLANES + tm, LANES), F32),
                        pltpu.VMEM((nf, 2 * fc // LANES, SUBLANES, LANES), F32),
                        pltpu.VMEM((tm, d), F32)],
        compiler_params=_params(),
        name="conv_ffn",
    )(x, mod, w_up, conv_w, conv_b, w_down, ln_g.reshape(1, d), ln_b.reshape(1, d))


def _conf_kernel(x_ref, mod_ref, w1_ref, b1_ref, dw_ref, dwb_ref, lng_ref, lnb_ref, w2_ref, b2_ref,
                 g_ref, b_ref, o_ref, cbuf, ybuf, *, tm):
    i = pl.program_id(1)
    d = D_MODEL
    halo = CONF_HALO
    ngrp = d // LANES

    @pl.when(i == 0)
    def _():
        cbuf[:, 0:halo, :] = jnp.zeros((ngrp, halo, LANES), F32)

    x = x_ref[...]
    h = _modulate(x, mod_ref, 0).astype(BF16)
    per = MXU_DIM // LANES
    for c in range(d // MXU_DIM):
        lo, hi = c * MXU_DIM, (c + 1) * MXU_DIM
        a = _dot(h, w1_ref[:, lo:hi]) + b1_ref[:, lo:hi]
        g = _dot(h, w1_ref[:, d + lo:d + hi]) + b1_ref[:, d + lo:d + hi]
        u = a * _sigmoid(g)
        for s in range(per):
            cbuf[c * per + s, halo:halo + tm, :] = u[:, s * LANES:(s + 1) * LANES]

    first = halo - (CONV_WIDTH - 1)

    def conv_body(c, _):
        for r0 in range(0, tm, CONF_RB):
            acc = jnp.broadcast_to(dwb_ref[c], (CONF_RB, LANES))
            for k in range(CONV_WIDTH):
                acc = acc + dw_ref[c, k:k + 1, :] * cbuf[c, r0 + first + k:r0 + first + k + CONF_RB, :]
            ybuf[c, r0:r0 + CONF_RB, :] = acc
        cbuf[c, 0:halo, :] = cbuf[c, tm:tm + halo, :]
        return 0

    lax.fori_loop(0, ngrp, conv_body, 0)

    y = jnp.concatenate([ybuf[c] for c in range(ngrp)], axis=1)
    z = _silu(_ln_rows(y, lng_ref[...], lnb_ref[...], LN_EPS))
    out = _dot(z.astype(BF16), w2_ref[...]) + b2_ref[...]
    o_ref[...] = _post_norm(x, out, mod_ref, 0, g_ref, b_ref)


def _conf_call(x, mod, w_pw1, b_pw1, dw_w, dw_b, ln_g, ln_b, w_pw2, b_pw2, pg, pb):
    bsz, seq, d = x.shape
    tm = min(CONF_TM, seq)
    ngrp = d // LANES
    dw = dw_w.reshape(CONV_WIDTH, ngrp, LANES).transpose(1, 0, 2)
    dwb = dw_b.reshape(ngrp, 1, LANES)
    kern = functools.partial(_conf_kernel, tm=tm)
    row = lambda v: v.reshape(1, -1)
    return pl.pallas_call(
        kern,
        out_shape=jax.ShapeDtypeStruct(x.shape, F32),
        grid=(bsz, seq // tm),
        in_specs=[_tile_spec(tm), _mod_spec(),
                  _const_spec((d, 2 * d)), _const_spec((1, 2 * d)),
                  _const_spec(dw.shape), _const_spec(dwb.shape),
                  _const_spec((1, d)), _const_spec((1, d)),
                  _const_spec((d, d)), _const_spec((1, d)),
                  _const_spec((1, d)), _const_spec((1, d))],
        out_specs=_tile_spec(tm),
        scratch_shapes=[pltpu.VMEM((ngrp, CONF_HALO + tm, LANES), F32),
                        pltpu.VMEM((ngrp, tm, LANES), F32)],
        compiler_params=_params(),
        name="conformer_conv",
    )(x, mod, w_pw1.astype(BF16), row(b_pw1), dw, dwb, row(ln_g), row(ln_b),
      w_pw2.astype(BF16), row(b_pw2), row(pg), row(pb))


def _hgrn_kernel(x_ref, mod_ref, lb_ref, win_ref, ng_ref, wout_ref, tri_ref, g_ref, b_ref, o_ref,
                 st_ref, obuf, *, tm, layer):
    i = pl.program_id(1)
    d = D_MODEL
    ch = HGRN_CHUNK
    nch = tm // ch

    @pl.when(i == 0)
    def _():
        st_ref[...] = jnp.zeros_like(st_ref)

    x = x_ref[...]
    h = _modulate(x, mod_ref, 0).astype(BF16)
    q = _silu(_dot(h, win_ref[:, 0:d]))
    logits = lb_ref[...]
    e = jnp.exp(logits - jnp.max(logits, axis=0, keepdims=True))
    sm = e / jnp.sum(e, axis=0, keepdims=True)
    lb = jnp.sum(sm[1:layer + 1, :], axis=0, keepdims=True) if layer > 0 else jnp.zeros((1, d), F32)
    f = lb + (1.0 - lb) * _sigmoid(_dot(h, win_ref[:, d:2 * d]))
    k = 1.0 - f
    lf = jnp.log(f)
    tri = tri_ref[...]
    bcum = sum(_dot(tri, part) for part in _split3(lf))
    b3 = bcum.reshape(nch, ch, d)
    b_mid = b3[:, ch // 2 - 1:ch // 2, :]
    b_last = b3[:, ch - 1:ch, :]
    q3 = q.reshape(nch, ch, d)
    k3 = k.reshape(nch, ch, d)
    qs = (q3 * jnp.exp(b3 - b_mid)).reshape(tm, d).astype(BF16)
    ks = (k3 * jnp.exp(b_mid - b3)).reshape(tm, d).astype(BF16)
    qd = (q3 * jnp.exp(b3)).reshape(tm, d).astype(BF16)
    kd = (k3 * jnp.exp(b_last - b3)).reshape(tm, d).astype(BF16)
    dec = jnp.exp(b_last)
    vb = _dot(h, win_ref[:, 2 * d:3 * d]).astype(BF16)

    causal = tri.astype(F32) > 0.5

    heads = [slice(hh * HGRN_DK, (hh + 1) * HGRN_DK) for hh in range(HGRN_HEADS)]
    chunks = [slice(c * ch, (c + 1) * ch) for c in range(nch)]
    o_intra = []
    for sl in heads:
        sc = _dot_nt(qs[:, sl], ks[:, sl])
        o_intra.append(_dot(jnp.where(causal, sc, 0.0).astype(BF16), vb[:, sl]))
    kv = [[_dot_tn(vb[rows, sl], kd[rows, sl]) for sl in heads] for rows in chunks]
    st = [st_ref[hh] for hh in range(HGRN_HEADS)]
    parts = [[] for _ in heads]
    for c, rows in enumerate(chunks):
        for hh, sl in enumerate(heads):
            parts[hh].append(_dot_nt(qd[rows, sl], st[hh].astype(BF16)))
            st[hh] = dec[c, :, sl] * st[hh] + kv[c][hh]
    for hh, sl in enumerate(heads):
        st_ref[hh] = st[hh]
        o = o_intra[hh] + jnp.concatenate(parts[hh], axis=0)
        o = o * lax.rsqrt(jnp.mean(o * o, axis=-1, keepdims=True) + RMS_EPS) * ng_ref[...]
        obuf[:, sl] = o
    gate = _silu(_dot(h, win_ref[:, 3 * d:4 * d]))
    y = _dot((obuf[...] * gate).astype(BF16), wout_ref[...])
    o_ref[...] = _post_norm(x, y, mod_ref, 0, g_ref, b_ref)


def _hgrn_call(x, mod, layer, lb_logits, w_in, norm_g, w_out, pg, pb):
    bsz, seq, d = x.shape
    tm = min(HGRN_TM, seq)
    r = jnp.arange(tm)
    tri = ((r[:, None] // HGRN_CHUNK == r[None, :] // HGRN_CHUNK) & (r[None, :] <= r[:, None])).astype(BF16)
    kern = functools.partial(_hgrn_kernel, tm=tm, layer=layer)
    row = lambda v: v.reshape(1, -1)
    return pl.pallas_call(
        kern,
        out_shape=jax.ShapeDtypeStruct(x.shape, F32),
        grid=(bsz, seq // tm),
        in_specs=[_tile_spec(tm), _mod_spec(), _const_spec(lb_logits.shape),
                  _const_spec((d, 4 * d)), _const_spec((1, HGRN_DK)), _const_spec((d, d)),
                  _const_spec((tm, tm)), _const_spec((1, d)), _const_spec((1, d))],
        out_specs=_tile_spec(tm),
        scratch_shapes=[pltpu.VMEM((HGRN_HEADS, HGRN_DK, HGRN_DK), F32),
                        pltpu.VMEM((tm, d), F32)],
        compiler_params=_params(),
        name="hgrn2",
    )(x, mod, lb_logits.astype(F32), w_in.astype(BF16), row(norm_g), w_out.astype(BF16), tri, row(pg), row(pb))


def _mlstm_kernel(x_ref, mod_ref, wup_ref, cw_ref, cb_ref, wqk_ref, wv_ref, wg_ref, bg_ref,
                  ng_ref, sk_ref, wdn_ref, tri_ref, g_ref, b_ref, o_ref,
                  xbuf, qbuf, kbuf, vbuf, c_ref, n_ref, m_ref, hbuf, *, tm):
    i = pl.program_id(1)
    d = D_MODEL
    inner = MLSTM_INNER
    hd = MLSTM_HDIM
    nh = MLSTM_HEADS
    blk = MXU_DIM
    pad = SUBLANES

    @pl.when(i == 0)
    def _():
        c_ref[...] = jnp.zeros_like(c_ref)
        n_ref[...] = jnp.zeros_like(n_ref)
        m_ref[...] = jnp.zeros_like(m_ref)
        xbuf[:, 0:pad, :] = jnp.zeros((inner // LANES, pad, LANES), F32)

    x = x_ref[...]
    h = _modulate(x, mod_ref, 0).astype(BF16)
    xm = _dot(h, wup_ref[:, 0:inner])
    convs = []
    for g in range(inner // LANES):
        lanes = slice(g * LANES, (g + 1) * LANES)
        xbuf[g, pad:pad + tm, :] = xm[:, lanes]
        cv = cw_ref[MLSTM_CONV - 1:MLSTM_CONV, lanes] * xm[:, lanes] + cb_ref[:, lanes]
        for kk in range(MLSTM_CONV - 1):
            off = pad - (MLSTM_CONV - 1) + kk
            cv = cv + cw_ref[kk:kk + 1, lanes] * xbuf[g, off:off + tm, :]
        convs.append(cv)
        xbuf[g, 0:pad, :] = xm[tm - pad:tm, lanes]
    xc = _silu(jnp.concatenate(convs, axis=1))
    xcb = xc.astype(BF16)
    xmb = xm.astype(BF16)
    for j in range(inner // blk):
        cols = slice(j * blk, (j + 1) * blk)
        qk = _dot(xcb[:, cols], wqk_ref[j])
        qbuf[:, cols] = qk[:, 0:blk]
        kbuf[:, cols] = qk[:, blk:2 * blk]
        vbuf[:, cols] = _dot(xmb[:, cols], wv_ref[j])
    qf = qbuf[...]
    kf = kbuf[...]
    vf = vbuf[...]
    qb = qf.astype(BF16)
    vb = vf.astype(BF16)
    gates = (_dot(qb, wg_ref[0:inner, :]) + _dot(kf.astype(BF16), wg_ref[inner:2 * inner, :])
             + _dot(vb, wg_ref[2 * inner:3 * inner, :]) + bg_ref[...])
    log_f = jnp.minimum(gates, 0.0) - jnp.log(1.0 + jnp.exp(-jnp.abs(gates)))
    tri = tri_ref[...]
    bc = sum(_dot(tri, part) for part in _split3(log_f))
    gates_t = gates.T
    bc_t = bc.T
    row = lax.broadcasted_iota(jnp.int32, (tm, tm), 0)
    col = lax.broadcasted_iota(jnp.int32, (tm, tm), 1)
    causal = col <= row
    kscaled = kf * (hd ** -0.5)

    for hh in range(nh):
        sl = slice(hh * hd, (hh + 1) * hd)
        bt_col = bc[:, nh + hh:nh + hh + 1]
        it_col = gates[:, hh:hh + 1]
        bt_row = bc_t[nh + hh:nh + hh + 1, :]
        it_row = gates_t[hh:hh + 1, :]
        m_prev = m_ref[hh][0:1, 0:1]
        log_w = jnp.where(causal, bt_col - bt_row + it_row, -jnp.inf)
        log_inter = bt_col + m_prev
        m_t = jnp.maximum(jnp.max(log_w, axis=-1, keepdims=True), log_inter)
        kh = kscaled[:, sl]
        khb = kh.astype(BF16)
        qh = qf[:, sl]
        qhb = qb[:, sl]
        vhb = vb[:, sl]
        s_qk = _dot_nt(qhb, khb) * jnp.exp(log_w - m_t)
        w_inter = jnp.exp(log_inter - m_t)
        cm = c_ref[hh]
        nvec = n_ref[hh]
        num = _dot(s_qk.astype(BF16), vhb) + w_inter * _dot(qhb, cm.astype(BF16))
        den = jnp.sum(s_qk, axis=-1, keepdims=True) + w_inter * jnp.sum(qh * nvec, axis=-1, keepdims=True)
        h_out = num / jnp.maximum(jnp.abs(den), jnp.exp(-m_t))
        b_last = bt_col[tm - 1:tm, :]
        log_ws = b_last - bt_col + it_col
        m_new = jnp.maximum(b_last + m_prev, jnp.max(log_ws, axis=0, keepdims=True))
        ws = jnp.exp(log_ws - m_new)
        decay = jnp.exp(b_last + m_prev - m_new)
        kw = kh * ws
        c_ref[hh] = decay * cm + _dot_tn(kw.astype(BF16), vhb)
        n_ref[hh] = decay * nvec + jnp.sum(kw, axis=0, keepdims=True)
        m_ref[hh] = jnp.broadcast_to(m_new, (SUBLANES, LANES))
        mu = jnp.mean(h_out, axis=-1, keepdims=True)
        dlt = h_out - mu
        var = jnp.mean(dlt * dlt, axis=-1, keepdims=True)
        hn = dlt * lax.rsqrt(var + LN_EPS)
        z = _dot(h, wup_ref[:, inner + hh * hd:inner + (hh + 1) * hd])
        hbuf[:, sl] = ((hn * ng_ref[:, sl] + sk_ref[:, sl] * xc[:, sl]) * _silu(z)).astype(BF16)
    y = _dot(hbuf[...], wdn_ref[...])
    o_ref[...] = _post_norm(x, y, mod_ref, 0, g_ref, b_ref)


def _block_diag_tiles(w, tile):
    g, blk, _ = w.shape
    per = tile // blk
    w4 = w.reshape(g // per, per, blk, blk)
    eye = jnp.eye(per, dtype=w.dtype)
    t = jnp.einsum('npio,pq->npiqo', w4, eye)
    return t.reshape(g // per, tile, tile)


def _mlstm_call(x, mod, w_up, conv_w, conv_b, w_q, w_k, w_v, w_gates, b_gates, norm_g, skip, w_down, pg, pb):
    bsz, seq, d = x.shape
    tm = min(MLSTM_TM, seq)
    inner, nh, hd = MLSTM_INNER, MLSTM_HEADS, MLSTM_HDIM
    wq_t = _block_diag_tiles(w_q, MXU_DIM)
    wk_t = _block_diag_tiles(w_k, MXU_DIM)
    wqk = jnp.concatenate([wq_t, wk_t], axis=-1).astype(BF16)
    wv = _block_diag_tiles(w_v, MXU_DIM).astype(BF16)
    wg = jnp.pad(w_gates, ((0, 0), (0, LANES - 2 * nh))).astype(BF16)
    bg = jnp.pad(b_gates, (0, LANES - 2 * nh)).reshape(1, LANES)
    r = jnp.arange(tm)
    tri = (r[None, :] <= r[:, None]).astype(BF16)
    kern = functools.partial(_mlstm_kernel, tm=tm)
    row = lambda v: v.reshape(1, -1)
    return pl.pallas_call(
        kern,
        out_shape=jax.ShapeDtypeStruct(x.shape, F32),
        grid=(bsz, seq // tm),
        in_specs=[_tile_spec(tm), _mod_spec(),
                  _const_spec((d, 2 * inner)), _const_spec((MLSTM_CONV, inner)), _const_spec((1, inner)),
                  _const_spec(wqk.shape), _const_spec(wv.shape),
                  _const_spec(wg.shape), _const_spec((1, LANES)),
                  _const_spec((1, inner)), _const_spec((1, inner)), _const_spec((inner, d)),
                  _const_spec((tm, tm)), _const_spec((1, d)), _const_spec((1, d))],
        out_specs=_tile_spec(tm),
        scratch_shapes=[pltpu.VMEM((inner // LANES, SUBLANES + tm, LANES), F32),
                        pltpu.VMEM((tm, inner), F32), pltpu.VMEM((tm, inner), F32),
                        pltpu.VMEM((tm, inner), F32),
                        pltpu.VMEM((nh, hd, hd), F32), pltpu.VMEM((nh, 1, hd), F32),
                        pltpu.VMEM((nh, SUBLANES, LANES), F32),
                        pltpu.VMEM((tm, inner), BF16)],
        compiler_params=_params(),
        name="mlstm",
    )(x, mod, w_up.astype(BF16), conv_w, row(conv_b), wqk, wv, wg, bg,
      row(norm_g), row(skip), w_down.astype(BF16), tri, row(pg), row(pb))


def _sb_kernel(x_ref, mod_ref, wqkv_ref, wout_ref, g_ref, b_ref, o_ref,
               q_scr, k_scr, v_scr, z_a, z_b, w_a, w_b, r_a, r_b, pav, later_scr, acc_scr, bias_scr, cum_scr,
               *, t):
    i = pl.program_id(1)
    d = D_MODEL
    nhp = d // LANES
    nblk = i + 1
    x = x_ref[...]
    h = _modulate(x, mod_ref, 0).astype(BF16)
    lane = lax.broadcasted_iota(jnp.int32, (1, LANES), 1)
    head_masks = (lane < SB_HDIM, lane >= SB_HDIM)
    base = pl.multiple_of(i * t, t)
    q = _dot(h, wqkv_ref[:, 0:d]) * (SB_HDIM ** -0.5 * LOG2_E)
    kk = _dot(h, wqkv_ref[:, d:2 * d])
    vv = _dot(h, wqkv_ref[:, 2 * d:3 * d])
    for hp in range(nhp):
        sl = slice(hp * LANES, (hp + 1) * LANES)
        k_scr[hp, pl.ds(base, t), :] = kk[:, sl].astype(BF16)
        for s in range(2):
            q_scr[hp, s * t:(s + 1) * t, :] = jnp.where(head_masks[s], q[:, sl], 0.0).astype(BF16)
            v_scr[hp, s, pl.ds(base, t), :] = jnp.where(head_masks[s], vv[:, sl], 0.0).astype(BF16)

    row = lax.broadcasted_iota(jnp.int32, (t, t), 0)
    col = lax.broadcasted_iota(jnp.int32, (t, t), 1)
    bias_scr[0] = jnp.zeros((t, t), F32)
    bias_scr[1] = jnp.where(col < row, 0.0, SB_MASKED)
    incl_ones = jnp.where(row >= col, 1.0, 0.0).astype(BF16)
    cum_scr[0:t, :] = incl_ones
    cum_scr[t:2 * t, :] = incl_ones
    pav[...] = jnp.zeros_like(pav)
    later_scr[...] = jnp.zeros_like(later_scr)
    acc_scr[...] = jnp.zeros_like(acc_scr)

    def scores(item, zbuf):
        hp, jj = item
        start = pl.multiple_of((i - jj) * t, t)
        z = _dot_nt(q_scr[hp], k_scr[hp, pl.ds(start, t), :])
        zbuf[...] = (z.reshape(2, t, t) + bias_scr[jnp.where(jj == 0, 1, 0)]).reshape(2 * t, t)

    def cumulate(zbuf, wbuf, rbuf):
        z = zbuf[...]
        sp = jnp.maximum(z, 0.0) + jnp.log(1.0 + jnp.exp2(-jnp.abs(z))) * LOG2_E
        hi = sp.astype(BF16)
        lo = (sp - hi.astype(F32)).astype(BF16)
        incl = _dot(jnp.concatenate([hi, lo], axis=1), cum_scr[...])
        wbuf[...] = z - incl
        rbuf[...] = jnp.broadcast_to(incl[:, 0:1], (2 * t, LANES))

    def weights(item, wbuf, rbuf):
        hp, jj = item
        later = jnp.where(jj == 0, 0.0, later_scr[...])
        a = jnp.exp2(wbuf[...] - jnp.concatenate([later] * (t // LANES), axis=1)).astype(BF16)
        later_scr[...] = later + rbuf[...]
        start = pl.multiple_of((i - jj) * t, t)
        vst = jnp.concatenate([v_scr[hp, 0, pl.ds(start, t), :], v_scr[hp, 1, pl.ds(start, t), :]], axis=0)
        return _dot(jnp.concatenate([a[0:t], a[t:2 * t]], axis=1), vst)

    def advance(item):
        hp, jj = item
        last = jj + 1 >= nblk
        return (jnp.where(last, jnp.minimum(hp + 1, nhp - 1), hp), jnp.where(last, 0, jj + 1))

    first = (jnp.int32(0), jnp.int32(0))
    second = advance(first)
    scores(first, z_a)
    scores(second, z_b)
    cumulate(z_a, w_a, r_a)

    def body(_, carry):
        hp_prev, it0 = carry[0], carry[1:]
        it1 = advance(it0)
        it2 = advance(it1)
        it3 = advance(it2)
        acc_scr[hp_prev] += pav[...]
        p0 = weights(it0, w_a, r_a)
        cumulate(z_b, w_b, r_b)
        scores(it2, z_a)
        acc_scr[it0[0]] += p0
        pav[...] = weights(it1, w_b, r_b)
        cumulate(z_a, w_a, r_a)
        scores(it3, z_b)
        return (it1[0],) + it2

    final = lax.fori_loop(0, (nhp // 2) * nblk, body, (jnp.int32(0),) + first)
    acc_scr[final[0]] += pav[...]
    o = jnp.concatenate([acc_scr[hp] for hp in range(nhp)], axis=1).astype(BF16)
    y = _dot(o, wout_ref[...])
    o_ref[...] = _post_norm(x, y, mod_ref, 0, g_ref, b_ref)


def _sb_call(x, mod, w_qkv, w_out, pg, pb):
    bsz, seq, d = x.shape
    t = min(SB_T, seq)
    nhp = d // LANES
    kern = functools.partial(_sb_kernel, t=t)
    row = lambda v: v.reshape(1, -1)
    return pl.pallas_call(
        kern,
        out_shape=jax.ShapeDtypeStruct(x.shape, F32),
        grid=(bsz, seq // t),
        in_specs=[_tile_spec(t), _mod_spec(), _const_spec((d, 3 * d)), _const_spec((d, d)),
                  _const_spec((1, d)), _const_spec((1, d))],
        out_specs=_tile_spec(t),
        scratch_shapes=[pltpu.VMEM((nhp, 2 * t, LANES), BF16),
                        pltpu.VMEM((nhp, seq, LANES), BF16),
                        pltpu.VMEM((nhp, 2, seq, LANES), BF16),
                        pltpu.VMEM((2 * t, t), F32), pltpu.VMEM((2 * t, t), F32),
                        pltpu.VMEM((2 * t, t), F32), pltpu.VMEM((2 * t, t), F32),
                        pltpu.VMEM((2 * t, LANES), F32), pltpu.VMEM((2 * t, LANES), F32),
                        pltpu.VMEM((t, LANES), F32),
                        pltpu.VMEM((2 * t, LANES), F32),
                        pltpu.VMEM((nhp, t, LANES), F32),
                        pltpu.VMEM((2, t, t), F32),
                        pltpu.VMEM((2 * t, t), BF16)],
        compiler_params=_params(),
        name="stick_breaking",
    )(x, mod, w_qkv.astype(BF16), w_out.astype(BF16), row(pg), row(pb))


def kernel(x, c, ada_w, ada_b, post_ln_g, post_ln_b, ffn_w_up, ffn_conv_w, ffn_conv_b, ffn_w_down, cc_w_pw1, cc_b_pw1, cc_dw_w, cc_dw_b, cc_ln_g, cc_ln_b, cc_w_pw2, cc_b_pw2, hg_lb_logits, hg_w_in, hg_norm_g, hg_w_out, ml_w_up, ml_conv_w, ml_conv_b, ml_w_q, ml_w_k, ml_w_v, ml_w_gates, ml_b_gates, ml_norm_g, ml_skip, ml_w_down, sb_w_qkv, sb_w_out):
    bsz = x.shape[0]
    mods = _ada_call(c, ada_w, ada_b).reshape(DEPTH, bsz, 6, D_MODEL)
    ffn_wup_b = ffn_w_up.astype(BF16)
    ffn_wdn_b = ffn_w_down.astype(BF16)
    ffn_cb = ffn_conv_b.reshape(DEPTH, 1, -1)
    for i in range(DEPTH):
        kind, j = i % 4, i // 4
        mod = mods[i]
        pg, pb = post_ln_g[i, 0], post_ln_b[i, 0]
        if kind == 0:
            x = _conf_call(x, mod, cc_w_pw1[j], cc_b_pw1[j], cc_dw_w[j], cc_dw_b[j], cc_ln_g[j], cc_ln_b[j],
                           cc_w_pw2[j], cc_b_pw2[j], pg, pb)
        elif kind == 1:
            x = _hgrn_call(x, mod, i, hg_lb_logits, hg_w_in[j], hg_norm_g[j], hg_w_out[j], pg, pb)
        elif kind == 2:
            x = _mlstm_call(x, mod, ml_w_up[j], ml_conv_w[j], ml_conv_b[j], ml_w_q[j], ml_w_k[j], ml_w_v[j],
                            ml_w_gates[j], ml_b_gates[j], ml_norm_g[j], ml_skip[j], ml_w_down[j], pg, pb)
        else:
            x = _sb_call(x, mod, sb_w_qkv[j], sb_w_out[j], pg, pb)
        x = _ffn_call(x, mod, i, ffn_wup_b, ffn_conv_w, ffn_cb, ffn_wdn_b, post_ln_g[i, 1], post_ln_b[i, 1])
    return x
```

```python
import functools

import jax
import jax.numpy as jnp
from jax import lax
from jax.experimental import pallas as pl
from jax.experimental.pallas import tpu as pltpu

F32 = jnp.float32
BF16 = jnp.bfloat16

D_MODEL = 1024
DEPTH = 4
D_FF = 2816
FFN_CONV = 3
CONV_WIDTH = 31
HGRN_HEADS = 8
HGRN_DK = 128
HGRN_CHUNK = 32
MLSTM_INNER = 2048
MLSTM_HEADS = 4
MLSTM_HDIM = 512
MLSTM_CONV = 4
MLSTM_QKV_BLOCK = 4
SB_HEADS = 16
SB_HDIM = 64
LOG2_E = 1.4426950408889634
SB_MASKED = -1e30
DN_ALPHA = (2.0 * DEPTH) ** 0.25
LN_EPS = 1e-5
RMS_EPS = 1e-6

VMEM_LIMIT_BYTES = 56 * 1024 * 1024
LANES = 128
SUBLANES = 8
MXU_DIM = 256

FFN_TM = 512
FFN_FC = 256
CONF_TM = 512
CONF_RB = 256
CONF_HALO = 32
HGRN_TM = 256
MLSTM_TM = 256
SB_T = 256


def _sigmoid(x):
    return 1.0 / (1.0 + jnp.exp(-x))


def _silu(x):
    return x * _sigmoid(x)


def _dot(a, b):
    return jnp.dot(a, b, preferred_element_type=F32)


def _dot_nt(a, b):
    return lax.dot_general(a, b, (((1,), (1,)), ((), ())), preferred_element_type=F32)


def _dot_tn(a, b):
    return lax.dot_general(a, b, (((0,), (0,)), ((), ())), preferred_element_type=F32)


def _split3(x):
    hi = x.astype(BF16)
    r1 = x - hi.astype(F32)
    mid = r1.astype(BF16)
    lo = (r1 - mid.astype(F32)).astype(BF16)
    return hi, mid, lo


def _ln_rows(r, g, b, eps):
    mu = jnp.mean(r, axis=-1, keepdims=True)
    d = r - mu
    var = jnp.mean(d * d, axis=-1, keepdims=True)
    return d * lax.rsqrt(var + eps) * g + b


def _modulate(x, mod_ref, which):
    base = 3 * which
    return x * (1.0 + mod_ref[base + 1:base + 2, :]) + mod_ref[base:base + 1, :]


def _post_norm(x, y, mod_ref, which, g_ref, b_ref):
    gate = mod_ref[3 * which + 2:3 * which + 3, :]
    return _ln_rows(DN_ALPHA * x + (1.0 + gate) * y, g_ref[...], b_ref[...], LN_EPS)


def _const_spec(shape):
    nd = len(shape)
    return pl.BlockSpec(shape, lambda *_: (0,) * nd, pipeline_mode=pl.Buffered(1))


def _tile_spec(tm):
    return pl.BlockSpec((None, tm, D_MODEL), lambda b, i: (b, i, 0))


def _mod_spec():
    return pl.BlockSpec((None, 6, D_MODEL), lambda b, i: (b, 0, 0))


def _params():
    return pltpu.CompilerParams(dimension_semantics=("arbitrary", "arbitrary"),
                                vmem_limit_bytes=VMEM_LIMIT_BYTES)


def _ada_kernel(c_ref, w_ref, b_ref, o_ref):
    o_ref[...] = _dot(_silu(c_ref[...]), w_ref[...]) + b_ref[...]


def _ada_call(c, ada_w, ada_b):
    depth, d, n = ada_w.shape
    bsz = c.shape[0]
    tn = 1536
    return pl.pallas_call(
        _ada_kernel,
        out_shape=jax.ShapeDtypeStruct((depth, bsz, n), F32),
        grid=(depth, n // tn),
        in_specs=[pl.BlockSpec((bsz, d), lambda l, j: (0, 0)),
                  pl.BlockSpec((None, d, tn), lambda l, j: (l, 0, j)),
                  pl.BlockSpec((None, 1, tn), lambda l, j: (l, 0, j))],
        out_specs=pl.BlockSpec((None, bsz, tn), lambda l, j: (l, 0, j)),
        compiler_params=_params(),
        name="ada_mod",
    )(c, ada_w, ada_b.reshape(depth, 1, n))


def _ffn_kernel(x_ref, mod_ref, wup_ref, cw_ref, cb_ref, wdn_ref, g_ref, b_ref, o_ref,
                hbuf, ubuf_a, ubuf_b, carry, acc_ref, *, tm, fc, nf):
    i = pl.program_id(1)

    @pl.when(i == 0)
    def _():
        carry[...] = jnp.zeros_like(carry)

    x = x_ref[...]
    hbuf[...] = _modulate(x, mod_ref, 1).astype(BF16)
    acc_ref[...] = jnp.zeros_like(acc_ref)

    ng = 2 * fc // LANES

    f = nf * fc
    half = ng // 2

    def cols(j, g):
        base = (0 if g < half else f) + j * fc + (g % half) * LANES
        return slice(base, base + LANES)

    def up(j, ubuf):
        ubuf[:, 0:SUBLANES, :] = carry[j]
        for part in range(2):
            u = _dot(hbuf[...], wup_ref[:, part * f + j * fc:part * f + (j + 1) * fc])
            for s in range(half):
                ubuf[part * half + s, SUBLANES:SUBLANES + tm, :] = u[:, s * LANES:(s + 1) * LANES]

    def down(j, ubuf):
        carry[j] = ubuf[:, tm:tm + SUBLANES, :]
        ys = []
        for g in range(ng):
            y = cb_ref[:, cols(j, g)]
            for k in range(FFN_CONV):
                off = SUBLANES - (FFN_CONV - 1) + k
                y = y + cw_ref[k:k + 1, cols(j, g)] * ubuf[g, off:off + tm, :]
            ys.append(y)
        gate = jnp.concatenate(ys[:half], axis=1)
        val = jnp.concatenate(ys[half:], axis=1)
        acc_ref[...] += _dot((_silu(gate) * val).astype(BF16), wdn_ref[j * fc:(j + 1) * fc, :])

    bufs = (ubuf_a, ubuf_b)
    up(0, bufs[0])
    for j in range(nf):
        if j + 1 < nf:
            up(j + 1, bufs[(j + 1) % 2])
        down(j, bufs[j % 2])
    o_ref[...] = _post_norm(x, acc_ref[...], mod_ref, 1, g_ref, b_ref)


def _layer_spec(shape, layer):
    rest = tuple(shape[1:])
    return pl.BlockSpec((None,) + rest, lambda *_: (layer,) + (0,) * len(rest), pipeline_mode=pl.Buffered(1))


def _ffn_call(x, mod, layer, w_up, conv_w, conv_b, w_down, ln_g, ln_b):
    bsz, seq, d = x.shape
    f = w_down.shape[1]
    tm, fc = min(FFN_TM, seq), FFN_FC
    nf = f // fc
    kern = functools.partial(_ffn_kernel, tm=tm, fc=fc, nf=nf)
    return pl.pallas_call(
        kern,
        out_shape=jax.ShapeDtypeStruct(x.shape, F32),
        grid=(bsz, seq // tm),
        in_specs=[_tile_spec(tm), _mod_spec(),
                  _layer_spec(w_up.shape, layer), _layer_spec(conv_w.shape, layer),
                  _layer_spec(conv_b.shape, layer), _layer_spec(w_down.shape, layer),
                  _const_spec((1, d)), _const_spec((1, d))],
        out_specs=_tile_spec(tm),
        scratch_shapes=[pltpu.VMEM((tm, d), BF16),
                        pltpu.VMEM((2 * fc // LANES, SUBLANES + tm, LANES), F32),
                        pltpu.VMEM((2 * fc // LANES, SUBLANES + tm, LANES), F32),
                        pltpu.VMEM((nf, 2 * fc // LANES, SUBLANES, LANES), F32),
                        pltpu.VMEM((tm, d), F32)],
        compiler_params=_params(),
        name="conv_ffn",
    )(x, mod, w_up, conv_w, conv_b, w_down, ln_g.reshape(1, d), ln_b.reshape(1, d))


def _conf_kernel(x_ref, mod_ref, w1_ref, b1_ref, dw_ref, dwb_ref, lng_ref, lnb_ref, w2_ref, b2_ref,
                 g_ref, b_ref, o_ref, cbuf, ybuf, *, tm):
    i = pl.program_id(1)
    d = D_MODEL
    halo = CONF_HALO
    ngrp = d // LANES

    @pl.when(i == 0)
    def _():
        cbuf[:, 0:halo, :] = jnp.zeros((ngrp, halo, LANES), F32)

    x = x_ref[...]
    h = _modulate(x, mod_ref, 0).astype(BF16)
    per = MXU_DIM // LANES
    first = halo - (CONV_WIDTH - 1)

    def glu_chunk(c):
        lo, hi = c * MXU_DIM, (c + 1) * MXU_DIM
        a = _dot(h, w1_ref[:, lo:hi]) + b1_ref[:, lo:hi]
        g = _dot(h, w1_ref[:, d + lo:d + hi]) + b1_ref[:, d + lo:d + hi]
        u = a * _sigmoid(g)
        for s in range(per):
            cbuf[c * per + s, halo:halo + tm, :] = u[:, s * LANES:(s + 1) * LANES]

    def conv_group(c):
        for r0 in range(0, tm, CONF_RB):
            acc = jnp.broadcast_to(dwb_ref[c], (CONF_RB, LANES))
            for k in range(CONV_WIDTH):
                acc = acc + dw_ref[c, k:k + 1, :] * cbuf[c, r0 + first + k:r0 + first + k + CONF_RB, :]
            ybuf[c, r0:r0 + CONF_RB, :] = acc
        cbuf[c, 0:halo, :] = cbuf[c, tm:tm + halo, :]

    nchunk = d // MXU_DIM
    glu_chunk(0)
    for c in range(nchunk):
        if c + 1 < nchunk:
            glu_chunk(c + 1)
        for s in range(per):
            conv_group(c * per + s)

    y = jnp.concatenate([ybuf[c] for c in range(ngrp)], axis=1)
    z = _silu(_ln_rows(y, lng_ref[...], lnb_ref[...], LN_EPS))
    out = _dot(z.astype(BF16), w2_ref[...]) + b2_ref[...]
    o_ref[...] = _post_norm(x, out, mod_ref, 0, g_ref, b_ref)


def _conf_call(x, mod, w_pw1, b_pw1, dw_w, dw_b, ln_g, ln_b, w_pw2, b_pw2, pg, pb):
    bsz, seq, d = x.shape
    tm = min(CONF_TM, seq)
    ngrp = d // LANES
    dw = dw_w.reshape(CONV_WIDTH, ngrp, LANES).transpose(1, 0, 2)
    dwb = dw_b.reshape(ngrp, 1, LANES)
    kern = functools.partial(_conf_kernel, tm=tm)
    row = lambda v: v.reshape(1, -1)
    return pl.pallas_call(
        kern,
        out_shape=jax.ShapeDtypeStruct(x.shape, F32),
        grid=(bsz, seq // tm),
        in_specs=[_tile_spec(tm), _mod_spec(),
                  _const_spec((d, 2 * d)), _const_spec((1, 2 * d)),
                  _const_spec(dw.shape), _const_spec(dwb.shape),
                  _const_spec((1, d)), _const_spec((1, d)),
                  _const_spec((d, d)), _const_spec((1, d)),
                  _const_spec((1, d)), _const_spec((1, d))],
        out_specs=_tile_spec(tm),
        scratch_shapes=[pltpu.VMEM((ngrp, CONF_HALO + tm, LANES), F32),
                        pltpu.VMEM((ngrp, tm, LANES), F32)],
        compiler_params=_params(),
        name="conformer_conv",
    )(x, mod, w_pw1.astype(BF16), row(b_pw1), dw, dwb, row(ln_g), row(ln_b),
      w_pw2.astype(BF16), row(b_pw2), row(pg), row(pb))


def _hgrn_kernel(x_ref, mod_ref, lb_ref, win_ref, ng_ref, wout_ref, tri_ref, g_ref, b_ref, o_ref,
                 st_ref, obuf, *, tm, layer):
    i = pl.program_id(1)
    d = D_MODEL
    ch = HGRN_CHUNK
    nch = tm // ch

    @pl.when(i == 0)
    def _():
        st_ref[...] = jnp.zeros_like(st_ref)

    x = x_ref[...]
    h = _modulate(x, mod_ref, 0).astype(BF16)
    q = _silu(_dot(h, win_ref[:, 0:d]))
    logits = lb_ref[...]
    e = jnp.exp(logits - jnp.max(logits, axis=0, keepdims=True))
    sm = e / jnp.sum(e, axis=0, keepdims=True)
    lb = jnp.sum(sm[1:layer + 1, :], axis=0, keepdims=True) if layer > 0 else jnp.zeros((1, d), F32)
    f = lb + (1.0 - lb) * _sigmoid(_dot(h, win_ref[:, d:2 * d]))
    k = 1.0 - f
    lf = jnp.log(f)
    tri = tri_ref[...]
    bcum = sum(_dot(tri, part) for part in _split3(lf))
    b3 = bcum.reshape(nch, ch, d)
    b_mid = b3[:, ch // 2 - 1:ch // 2, :]
    b_last = b3[:, ch - 1:ch, :]
    q3 = q.reshape(nch, ch, d)
    k3 = k.reshape(nch, ch, d)
    qs = (q3 * jnp.exp(b3 - b_mid)).reshape(tm, d).astype(BF16)
    ks = (k3 * jnp.exp(b_mid - b3)).reshape(tm, d).astype(BF16)
    qd = (q3 * jnp.exp(b3)).reshape(tm, d).astype(BF16)
    kd = (k3 * jnp.exp(b_last - b3)).reshape(tm, d).astype(BF16)
    dec = jnp.exp(b_last)
    vb = _dot(h, win_ref[:, 2 * d:3 * d]).astype(BF16)

    causal = tri.astype(F32) > 0.5

    heads = [slice(hh * HGRN_DK, (hh + 1) * HGRN_DK) for hh in range(HGRN_HEADS)]
    chunks = [slice(c * ch, (c + 1) * ch) for c in range(nch)]
    o_intra = []
    for sl in heads:
        sc = _dot_nt(qs[:, sl], ks[:, sl])
        o_intra.append(_dot(jnp.where(causal, sc, 0.0).astype(BF16), vb[:, sl]))
    kv = [[_dot_tn(vb[rows, sl], kd[rows, sl]) for sl in heads] for rows in chunks]
    st = [st_ref[hh] for hh in range(HGRN_HEADS)]
    parts = [[] for _ in heads]
    for c, rows in enumerate(chunks):
        for hh, sl in enumerate(heads):
            parts[hh].append(_dot_nt(qd[rows, sl], st[hh].astype(BF16)))
            st[hh] = dec[c, :, sl] * st[hh] + kv[c][hh]
    for hh, sl in enumerate(heads):
        st_ref[hh] = st[hh]
        o = o_intra[hh] + jnp.concatenate(parts[hh], axis=0)
        o = o * lax.rsqrt(jnp.mean(o * o, axis=-1, keepdims=True) + RMS_EPS) * ng_ref[...]
        obuf[:, sl] = o
    gate = _silu(_dot(h, win_ref[:, 3 * d:4 * d]))
    y = _dot((obuf[...] * gate).astype(BF16), wout_ref[...])
    o_ref[...] = _post_norm(x, y, mod_ref, 0, g_ref, b_ref)


def _hgrn_call(x, mod, layer, lb_logits, w_in, norm_g, w_out, pg, pb):
    bsz, seq, d = x.shape
    tm = min(HGRN_TM, seq)
    r = jnp.arange(tm)
    tri = ((r[:, None] // HGRN_CHUNK == r[None, :] // HGRN_CHUNK) & (r[None, :] <= r[:, None])).astype(BF16)
    kern = functools.partial(_hgrn_kernel, tm=tm, layer=layer)
    row = lambda v: v.reshape(1, -1)
    return pl.pallas_call(
        kern,
        out_shape=jax.ShapeDtypeStruct(x.shape, F32),
        grid=(bsz, seq // tm),
        in_specs=[_tile_spec(tm), _mod_spec(), _const_spec(lb_logits.shape),
                  _const_spec((d, 4 * d)), _const_spec((1, HGRN_DK)), _const_spec((d, d)),
                  _const_spec((tm, tm)), _const_spec((1, d)), _const_spec((1, d))],
        out_specs=_tile_spec(tm),
        scratch_shapes=[pltpu.VMEM((HGRN_HEADS, HGRN_DK, HGRN_DK), F32),
                        pltpu.VMEM((tm, d), F32)],
        compiler_params=_params(),
        name="hgrn2",
    )(x, mod, lb_logits.astype(F32), w_in.astype(BF16), row(norm_g), w_out.astype(BF16), tri, row(pg), row(pb))


def _mlstm_kernel(x_ref, mod_ref, wup_ref, cw_ref, cb_ref, wqk_ref, wv_ref, wg_ref, bg_ref,
                  ng_ref, sk_ref, wdn_ref, tri_ref, g_ref, b_ref, o_ref,
                  xbuf, qbuf, kbuf, vbuf, c_ref, n_ref, m_ref, hbuf, *, tm):
    i = pl.program_id(1)
    d = D_MODEL
    inner = MLSTM_INNER
    hd = MLSTM_HDIM
    nh = MLSTM_HEADS
    blk = MXU_DIM
    pad = SUBLANES

    @pl.when(i == 0)
    def _():
        c_ref[...] = jnp.zeros_like(c_ref)
        n_ref[...] = jnp.zeros_like(n_ref)
        m_ref[...] = jnp.zeros_like(m_ref)
        xbuf[:, 0:pad, :] = jnp.zeros((inner // LANES, pad, LANES), F32)

    x = x_ref[...]
    h = _modulate(x, mod_ref, 0).astype(BF16)
    xm = _dot(h, wup_ref[:, 0:inner])
    convs = []
    for g in range(inner // LANES):
        lanes = slice(g * LANES, (g + 1) * LANES)
        xbuf[g, pad:pad + tm, :] = xm[:, lanes]
        cv = cw_ref[MLSTM_CONV - 1:MLSTM_CONV, lanes] * xm[:, lanes] + cb_ref[:, lanes]
        for kk in range(MLSTM_CONV - 1):
            off = pad - (MLSTM_CONV - 1) + kk
            cv = cv + cw_ref[kk:kk + 1, lanes] * xbuf[g, off:off + tm, :]
        convs.append(cv)
        xbuf[g, 0:pad, :] = xm[tm - pad:tm, lanes]
    xc = _silu(jnp.concatenate(convs, axis=1))
    xcb = xc.astype(BF16)
    xmb = xm.astype(BF16)
    for j in range(inner // blk):
        cols = slice(j * blk, (j + 1) * blk)
        qk = _dot(xcb[:, cols], wqk_ref[j])
        qbuf[:, cols] = qk[:, 0:blk]
        kbuf[:, cols] = qk[:, blk:2 * blk]
        vbuf[:, cols] = _dot(xmb[:, cols], wv_ref[j])
    qf = qbuf[...]
    kf = kbuf[...]
    vf = vbuf[...]
    qb = qf.astype(BF16)
    vb = vf.astype(BF16)
    gates = (_dot(qb, wg_ref[0:inner, :]) + _dot(kf.astype(BF16), wg_ref[inner:2 * inner, :])
             + _dot(vb, wg_ref[2 * inner:3 * inner, :]) + bg_ref[...])
    log_f = jnp.minimum(gates, 0.0) - jnp.log(1.0 + jnp.exp(-jnp.abs(gates)))
    tri = tri_ref[...]
    bc = sum(_dot(tri, part) for part in _split3(log_f))
    gates_t = gates.T
    bc_t = bc.T
    row = lax.broadcasted_iota(jnp.int32, (tm, tm), 0)
    col = lax.broadcasted_iota(jnp.int32, (tm, tm), 1)
    causal = col <= row
    kscaled = kf * (hd ** -0.5)

    heads = [slice(hh * hd, (hh + 1) * hd) for hh in range(nh)]
    kh = [kscaled[:, sl] for sl in heads]
    cm = [c_ref[hh] for hh in range(nh)]
    qk = [_dot_nt(qb[:, sl], kh[hh].astype(BF16)) for hh, sl in enumerate(heads)]
    qc = [_dot(qb[:, sl], cm[hh].astype(BF16)) for hh, sl in enumerate(heads)]
    zg = [_dot(h, wup_ref[:, inner + sl.start:inner + sl.stop]) for sl in heads]
    for hh, sl in enumerate(heads):
        bt_col = bc[:, nh + hh:nh + hh + 1]
        it_col = gates[:, hh:hh + 1]
        bt_row = bc_t[nh + hh:nh + hh + 1, :]
        it_row = gates_t[hh:hh + 1, :]
        m_prev = m_ref[hh][0:1, 0:1]
        log_w = jnp.where(causal, bt_col - bt_row + it_row, -jnp.inf)
        log_inter = bt_col + m_prev
        m_t = jnp.maximum(jnp.max(log_w, axis=-1, keepdims=True), log_inter)
        vhb = vb[:, sl]
        s_qk = qk[hh] * jnp.exp(log_w - m_t)
        w_inter = jnp.exp(log_inter - m_t)
        nvec = n_ref[hh]
        b_last = bt_col[tm - 1:tm, :]
        log_ws = b_last - bt_col + it_col
        m_new = jnp.maximum(b_last + m_prev, jnp.max(log_ws, axis=0, keepdims=True))
        ws = jnp.exp(log_ws - m_new)
        decay = jnp.exp(b_last + m_prev - m_new)
        kw = kh[hh] * ws
        c_ref[hh] = decay * cm[hh] + _dot_tn(kw.astype(BF16), vhb)
        n_ref[hh] = decay * nvec + jnp.sum(kw, axis=0, keepdims=True)
        m_ref[hh] = jnp.broadcast_to(m_new, (SUBLANES, LANES))
        num = _dot(s_qk.astype(BF16), vhb) + w_inter * qc[hh]
        den = (jnp.sum(s_qk, axis=-1, keepdims=True)
               + w_inter * jnp.sum(qf[:, sl] * nvec, axis=-1, keepdims=True))
        h_out = num / jnp.maximum(jnp.abs(den), jnp.exp(-m_t))
        mu = jnp.mean(h_out, axis=-1, keepdims=True)
        dlt = h_out - mu
        var = jnp.mean(dlt * dlt, axis=-1, keepdims=True)
        hn = dlt * lax.rsqrt(var + LN_EPS)
        hbuf[:, sl] = ((hn * ng_ref[:, sl] + sk_ref[:, sl] * xc[:, sl]) * _silu(zg[hh])).astype(BF16)
    y = _dot(hbuf[...], wdn_ref[...])
    o_ref[...] = _post_norm(x, y, mod_ref, 0, g_ref, b_ref)


def _block_diag_tiles(w, tile):
    g, blk, _ = w.shape
    per = tile // blk
    rows = w.reshape(g // per, tile, blk)
    r = jnp.arange(tile)
    same_block = (r[:, None] // blk == r[None, :] // blk).astype(w.dtype)
    return jnp.tile(rows, (1, 1, per)) * same_block


def _mlstm_call(x, mod, w_up, conv_w, conv_b, w_q, w_k, w_v, w_gates, b_gates, norm_g, skip, w_down, pg, pb):
    bsz, seq, d = x.shape
    tm = min(MLSTM_TM, seq)
    inner, nh, hd = MLSTM_INNER, MLSTM_HEADS, MLSTM_HDIM
    wq_t = _block_diag_tiles(w_q, MXU_DIM)
    wk_t = _block_diag_tiles(w_k, MXU_DIM)
    wqk = jnp.concatenate([wq_t, wk_t], axis=-1).astype(BF16)
    wv = _block_diag_tiles(w_v, MXU_DIM).astype(BF16)
    wg = jnp.pad(w_gates, ((0, 0), (0, LANES - 2 * nh))).astype(BF16)
    bg = jnp.pad(b_gates, (0, LANES - 2 * nh)).reshape(1, LANES)
    r = jnp.arange(tm)
    tri = (r[None, :] <= r[:, None]).astype(BF16)
    kern = functools.partial(_mlstm_kernel, tm=tm)
    row = lambda v: v.reshape(1, -1)
    return pl.pallas_call(
        kern,
        out_shape=jax.ShapeDtypeStruct(x.shape, F32),
        grid=(bsz, seq // tm),
        in_specs=[_tile_spec(tm), _mod_spec(),
                  _const_spec((d, 2 * inner)), _const_spec((MLSTM_CONV, inner)), _const_spec((1, inner)),
                  _const_spec(wqk.shape), _const_spec(wv.shape),
                  _const_spec(wg.shape), _const_spec((1, LANES)),
                  _const_spec((1, inner)), _const_spec((1, inner)), _const_spec((inner, d)),
                  _const_spec((tm, tm)), _const_spec((1, d)), _const_spec((1, d))],
        out_specs=_tile_spec(tm),
        scratch_shapes=[pltpu.VMEM((inner // LANES, SUBLANES + tm, LANES), F32),
                        pltpu.VMEM((tm, inner), F32), pltpu.VMEM((tm, inner), F32),
                        pltpu.VMEM((tm, inner), F32),
                        pltpu.VMEM((nh, hd, hd), F32), pltpu.VMEM((nh, 1, hd), F32),
                        pltpu.VMEM((nh, SUBLANES, LANES), F32),
                        pltpu.VMEM((tm, inner), BF16)],
        compiler_params=_params(),
        name="mlstm",
    )(x, mod, w_up.astype(BF16), conv_w, row(conv_b), wqk, wv, wg, bg,
      row(norm_g), row(skip), w_down.astype(BF16), tri, row(pg), row(pb))


def _sb_kernel(x_ref, mod_ref, wqkv_ref, wout_ref, g_ref, b_ref, o_ref,
               q_scr, k_scr, v_scr, z_all, w_all, r_all, pav, later_scr, acc_scr, bias_scr, cum_scr,
               *, t):
    i = pl.program_id(1)
    d = D_MODEL
    nhp = d // LANES
    nblk = i + 1
    x = x_ref[...]
    h = _modulate(x, mod_ref, 0).astype(BF16)
    lane = lax.broadcasted_iota(jnp.int32, (1, LANES), 1)
    head_masks = (lane < SB_HDIM, lane >= SB_HDIM)
    base = pl.multiple_of(i * t, t)
    q = _dot(h, wqkv_ref[:, 0:d]) * (SB_HDIM ** -0.5 * LOG2_E)
    kk = _dot(h, wqkv_ref[:, d:2 * d])
    vv = _dot(h, wqkv_ref[:, 2 * d:3 * d])
    for hp in range(nhp):
        sl = slice(hp * LANES, (hp + 1) * LANES)
        k_scr[hp, pl.ds(base, t), :] = kk[:, sl].astype(BF16)
        for s in range(2):
            q_scr[hp, s * t:(s + 1) * t, :] = jnp.where(head_masks[s], q[:, sl], 0.0).astype(BF16)
            v_scr[hp, s, pl.ds(base, t), :] = jnp.where(head_masks[s], vv[:, sl], 0.0).astype(BF16)

    row = lax.broadcasted_iota(jnp.int32, (t, t), 0)
    col = lax.broadcasted_iota(jnp.int32, (t, t), 1)
    bias_scr[0] = jnp.zeros((t, t), F32)
    bias_scr[1] = jnp.where(col < row, 0.0, SB_MASKED)
    incl_ones = jnp.where(row >= col, 1.0, 0.0).astype(BF16)
    cum_scr[0:t, :] = incl_ones
    cum_scr[t:2 * t, :] = incl_ones
    pav[...] = jnp.zeros_like(pav)
    later_scr[...] = jnp.zeros_like(later_scr)
    acc_scr[...] = jnp.zeros_like(acc_scr)

    def scores(item, zbuf):
        hp, jj = item
        start = pl.multiple_of((i - jj) * t, t)
        z = _dot_nt(q_scr[hp], k_scr[hp, pl.ds(start, t), :])
        zbuf[...] = (z.reshape(2, t, t) + bias_scr[jnp.where(jj == 0, 1, 0)]).reshape(2 * t, t)

    def cumulate(zbuf, wbuf, rbuf):
        z = zbuf[...]
        sp = jnp.maximum(z, 0.0) + jnp.log(1.0 + jnp.exp2(-jnp.abs(z))) * LOG2_E
        hi = sp.astype(BF16)
        lo = (sp - hi.astype(F32)).astype(BF16)
        incl = _dot(jnp.concatenate([hi, lo], axis=1), cum_scr[...])
        wbuf[...] = z - incl
        rbuf[...] = jnp.broadcast_to(incl[:, 0:1], (2 * t, LANES))

    def weights(item, wbuf, rbuf):
        hp, jj = item
        later = jnp.where(jj == 0, 0.0, later_scr[...])
        a = jnp.exp2(wbuf[...] - jnp.concatenate([later] * (t // LANES), axis=1)).astype(BF16)
        later_scr[...] = later + rbuf[...]
        start = pl.multiple_of((i - jj) * t, t)
        vst = jnp.concatenate([v_scr[hp, 0, pl.ds(start, t), :], v_scr[hp, 1, pl.ds(start, t), :]], axis=0)
        return _dot(jnp.concatenate([a[0:t], a[t:2 * t]], axis=1), vst)

    def advance(item):
        hp, jj = item
        last = jj + 1 >= nblk
        return (jnp.where(last, jnp.minimum(hp + 1, nhp - 1), hp), jnp.where(last, 0, jj + 1))

    def step(hp_prev, items, src, dst, with_scores=True, with_cumulate=True):
        it0, it1, _, _, it4, it5 = items
        if with_scores:
            scores(it4, z_all.at[dst])
            scores(it5, z_all.at[dst + 1])
        acc_scr[hp_prev] += pav[...]
        p0 = weights(it0, w_all.at[src], r_all.at[src])
        pav[...] = weights(it1, w_all.at[src + 1], r_all.at[src + 1])
        acc_scr[it0[0]] += p0
        if with_cumulate:
            cumulate(z_all.at[src], w_all.at[dst], r_all.at[dst])
            cumulate(z_all.at[src + 1], w_all.at[dst + 1], r_all.at[dst + 1])

    def six(it0):
        items = [it0]
        for _ in range(5):
            items.append(advance(items[-1]))
        return items

    first = six((jnp.int32(0), jnp.int32(0)))
    scores(first[0], z_all.at[2])
    scores(first[1], z_all.at[3])
    cumulate(z_all.at[2], w_all.at[0], r_all.at[0])
    cumulate(z_all.at[3], w_all.at[1], r_all.at[1])
    scores(first[2], z_all.at[0])
    scores(first[3], z_all.at[1])

    def body(_, carry):
        items = six(carry[1:])
        step(carry[0], items, 0, 2)
        items2 = six(items[2])
        step(items[1][0], items2, 2, 0)
        return (items2[1][0],) + items2[2]

    trips = (nhp // 2) * nblk
    carry = lax.fori_loop(0, trips // 2 - 1, body, (jnp.int32(0),) + first[0])
    items = six(carry[1:])
    step(carry[0], items, 0, 2, with_scores=False)
    items2 = six(items[2])
    step(items[1][0], items2, 2, 0, with_scores=False, with_cumulate=False)
    acc_scr[items2[1][0]] += pav[...]
    o = jnp.concatenate([acc_scr[hp] for hp in range(nhp)], axis=1).astype(BF16)
    y = _dot(o, wout_ref[...])
    o_ref[...] = _post_norm(x, y, mod_ref, 0, g_ref, b_ref)


def _sb_call(x, mod, w_qkv, w_out, pg, pb):
    bsz, seq, d = x.shape
    t = min(SB_T, seq)
    nhp = d // LANES
    kern = functools.partial(_sb_kernel, t=t)
    row = lambda v: v.reshape(1, -1)
    return pl.pallas_call(
        kern,
        out_shape=jax.ShapeDtypeStruct(x.shape, F32),
        grid=(bsz, seq // t),
        in_specs=[_tile_spec(t), _mod_spec(), _const_spec((d, 3 * d)), _const_spec((d, d)),
                  _const_spec((1, d)), _const_spec((1, d))],
        out_specs=_tile_spec(t),
        scratch_shapes=[pltpu.VMEM((nhp, 2 * t, LANES), BF16),
                        pltpu.VMEM((nhp, seq, LANES), BF16),
                        pltpu.VMEM((nhp, 2, seq, LANES), BF16),
                        pltpu.VMEM((4, 2 * t, t), F32), pltpu.VMEM((4, 2 * t, t), F32),
                        pltpu.VMEM((4, 2 * t, LANES), F32),
                        pltpu.VMEM((t, LANES), F32),
                        pltpu.VMEM((2 * t, LANES), F32),
                        pltpu.VMEM((nhp, t, LANES), F32),
                        pltpu.VMEM((2, t, t), F32),
                        pltpu.VMEM((2 * t, t), BF16)],
        compiler_params=_params(),
        name="stick_breaking",
    )(x, mod, w_qkv.astype(BF16), w_out.astype(BF16), row(pg), row(pb))


def kernel(x, c, ada_w, ada_b, post_ln_g, post_ln_b, ffn_w_up, ffn_conv_w, ffn_conv_b, ffn_w_down, cc_w_pw1, cc_b_pw1, cc_dw_w, cc_dw_b, cc_ln_g, cc_ln_b, cc_w_pw2, cc_b_pw2, hg_lb_logits, hg_w_in, hg_norm_g, hg_w_out, ml_w_up, ml_conv_w, ml_conv_b, ml_w_q, ml_w_k, ml_w_v, ml_w_gates, ml_b_gates, ml_norm_g, ml_skip, ml_w_down, sb_w_qkv, sb_w_out):
    bsz = x.shape[0]
    mods = _ada_call(c, ada_w, ada_b).reshape(DEPTH, bsz, 6, D_MODEL)
    ffn_wup_b = ffn_w_up.astype(BF16)
    ffn_wdn_b = ffn_w_down.astype(BF16)
    ffn_cb = ffn_conv_b.reshape(DEPTH, 1, -1)
    for i in range(DEPTH):
        kind, j = i % 4, i // 4
        mod = mods[i]
        pg, pb = post_ln_g[i, 0], post_ln_b[i, 0]
        if kind == 0:
            x = _conf_call(x, mod, cc_w_pw1[j], cc_b_pw1[j], cc_dw_w[j], cc_dw_b[j], cc_ln_g[j], cc_ln_b[j],
                           cc_w_pw2[j], cc_b_pw2[j], pg, pb)
        elif kind == 1:
            x = _hgrn_call(x, mod, i, hg_lb_logits, hg_w_in[j], hg_norm_g[j], hg_w_out[j], pg, pb)
        elif kind == 2:
            x = _mlstm_call(x, mod, ml_w_up[j], ml_conv_w[j], ml_conv_b[j], ml_w_q[j], ml_w_k[j], ml_w_v[j],
                            ml_w_gates[j], ml_b_gates[j], ml_norm_g[j], ml_skip[j], ml_w_down[j], pg, pb)
        else:
            x = _sb_call(x, mod, sb_w_qkv[j], sb_w_out[j], pg, pb)
        x = _ffn_call(x, mod, i, ffn_wup_b, ffn_conv_w, ffn_cb, ffn_wdn_b, post_ln_g[i, 1], post_ln_b[i, 1])
    return x
```

```python
import functools

import jax
import jax.numpy as jnp
from jax import lax
from jax.experimental import pallas as pl
from jax.experimental.pallas import tpu as pltpu

F32 = jnp.float32
BF16 = jnp.bfloat16

D_MODEL = 1024
DEPTH = 4
D_FF = 2816
FFN_CONV = 3
CONV_WIDTH = 31
HGRN_HEADS = 8
HGRN_DK = 128
HGRN_CHUNK = 32
MLSTM_INNER = 2048
MLSTM_HEADS = 4
MLSTM_HDIM = 512
MLSTM_CONV = 4
MLSTM_QKV_BLOCK = 4
SB_HEADS = 16
SB_HDIM = 64
LOG2_E = 1.4426950408889634
SB_MASKED = -1e30
DN_ALPHA = (2.0 * DEPTH) ** 0.25
LN_EPS = 1e-5
RMS_EPS = 1e-6

VMEM_LIMIT_BYTES = 56 * 1024 * 1024
LANES = 128
SUBLANES = 8
MXU_DIM = 256

FFN_TM = 512
FFN_FC = 256
CONF_TM = 512
CONF_RB = 256
CONF_HALO = 32
HGRN_TM = 256
MLSTM_TM = 256
SB_T = 256


def _sigmoid(x):
    return 1.0 / (1.0 + jnp.exp(-x))


def _silu(x):
    return x * _sigmoid(x)


def _dot(a, b):
    return jnp.dot(a, b, preferred_element_type=F32)


def _dot_nt(a, b):
    return lax.dot_general(a, b, (((1,), (1,)), ((), ())), preferred_element_type=F32)


def _dot_tn(a, b):
    return lax.dot_general(a, b, (((0,), (0,)), ((), ())), preferred_element_type=F32)


def _split3(x):
    hi = x.astype(BF16)
    r1 = x - hi.astype(F32)
    mid = r1.astype(BF16)
    lo = (r1 - mid.astype(F32)).astype(BF16)
    return hi, mid, lo


def _ln_rows(r, g, b, eps):
    mu = jnp.mean(r, axis=-1, keepdims=True)
    d = r - mu
    var = jnp.mean(d * d, axis=-1, keepdims=True)
    return d * lax.rsqrt(var + eps) * g + b


def _modulate(x, mod_ref, which):
    base = 3 * which
    return x * (1.0 + mod_ref[base + 1:base + 2, :]) + mod_ref[base:base + 1, :]


def _post_norm(x, y, mod_ref, which, g_ref, b_ref):
    gate = mod_ref[3 * which + 2:3 * which + 3, :]
    return _ln_rows(DN_ALPHA * x + (1.0 + gate) * y, g_ref[...], b_ref[...], LN_EPS)


def _const_spec(shape):
    nd = len(shape)
    return pl.BlockSpec(shape, lambda *_: (0,) * nd, pipeline_mode=pl.Buffered(1))


def _tile_spec(tm):
    return pl.BlockSpec((None, tm, D_MODEL), lambda b, i: (b, i, 0))


def _mod_spec():
    return pl.BlockSpec((None, 6, D_MODEL), lambda b, i: (b, 0, 0))


def _params():
    return pltpu.CompilerParams(dimension_semantics=("arbitrary", "arbitrary"),
                                vmem_limit_bytes=VMEM_LIMIT_BYTES)


def _ada_kernel(c_ref, w_ref, b_ref, o_ref):
    o_ref[...] = _dot(_silu(c_ref[...]), w_ref[...]) + b_ref[...]


def _ada_call(c, ada_w, ada_b):
    depth, d, n = ada_w.shape
    bsz = c.shape[0]
    tn = 1536
    return pl.pallas_call(
        _ada_kernel,
        out_shape=jax.ShapeDtypeStruct((depth, bsz, n), F32),
        grid=(depth, n // tn),
        in_specs=[pl.BlockSpec((bsz, d), lambda l, j: (0, 0)),
                  pl.BlockSpec((None, d, tn), lambda l, j: (l, 0, j)),
                  pl.BlockSpec((None, 1, tn), lambda l, j: (l, 0, j))],
        out_specs=pl.BlockSpec((None, bsz, tn), lambda l, j: (l, 0, j)),
        compiler_params=_params(),
        name="ada_mod",
    )(c, ada_w, ada_b.reshape(depth, 1, n))


def _ffn_kernel(x_ref, mod_ref, wup_ref, cw_ref, cb_ref, wdn_ref, g_ref, b_ref, o_ref,
                hbuf, ubuf_a, ubuf_b, ubuf_c, carry, acc_ref, *, tm, fc, nf):
    i = pl.program_id(1)

    @pl.when(i == 0)
    def _():
        carry[...] = jnp.zeros_like(carry)

    x = x_ref[...]
    hbuf[...] = _modulate(x, mod_ref, 1).astype(BF16)
    acc_ref[...] = jnp.zeros_like(acc_ref)

    ng = 2 * fc // LANES

    f = nf * fc
    half = ng // 2

    def cols(j, g):
        base = (0 if g < half else f) + j * fc + (g % half) * LANES
        return slice(base, base + LANES)

    def up(j, ubuf):
        ubuf[:, 0:SUBLANES, :] = carry[j]
        for part in range(2):
            u = _dot(hbuf[...], wup_ref[:, part * f + j * fc:part * f + (j + 1) * fc])
            for s in range(half):
                ubuf[part * half + s, SUBLANES:SUBLANES + tm, :] = u[:, s * LANES:(s + 1) * LANES]

    def down(j, ubuf):
        carry[j] = ubuf[:, tm:tm + SUBLANES, :]
        ys = []
        for g in range(ng):
            y = cb_ref[:, cols(j, g)]
            for k in range(FFN_CONV):
                off = SUBLANES - (FFN_CONV - 1) + k
                y = y + cw_ref[k:k + 1, cols(j, g)] * ubuf[g, off:off + tm, :]
            ys.append(y)
        gate = jnp.concatenate(ys[:half], axis=1)
        val = jnp.concatenate(ys[half:], axis=1)
        acc_ref[...] += _dot((_silu(gate) * val).astype(BF16), wdn_ref[j * fc:(j + 1) * fc, :])

    bufs = (ubuf_a, ubuf_b, ubuf_c)
    ahead = len(bufs) - 1
    for j in range(min(ahead, nf)):
        up(j, bufs[j % len(bufs)])
    for j in range(nf):
        if j + ahead < nf:
            up(j + ahead, bufs[(j + ahead) % len(bufs)])
        down(j, bufs[j % len(bufs)])
    o_ref[...] = _post_norm(x, acc_ref[...], mod_ref, 1, g_ref, b_ref)


def _layer_spec(shape, layer):
    rest = tuple(shape[1:])
    return pl.BlockSpec((None,) + rest, lambda *_: (layer,) + (0,) * len(rest), pipeline_mode=pl.Buffered(1))


def _ffn_call(x, mod, layer, w_up, conv_w, conv_b, w_down, ln_g, ln_b):
    bsz, seq, d = x.shape
    f = w_down.shape[1]
    tm, fc = min(FFN_TM, seq), FFN_FC
    nf = f // fc
    kern = functools.partial(_ffn_kernel, tm=tm, fc=fc, nf=nf)
    return pl.pallas_call(
        kern,
        out_shape=jax.ShapeDtypeStruct(x.shape, F32),
        grid=(bsz, seq // tm),
        in_specs=[_tile_spec(tm), _mod_spec(),
                  _layer_spec(w_up.shape, layer), _layer_spec(conv_w.shape, layer),
                  _layer_spec(conv_b.shape, layer), _layer_spec(w_down.shape, layer),
                  _const_spec((1, d)), _const_spec((1, d))],
        out_specs=_tile_spec(tm),
        scratch_shapes=[pltpu.VMEM((tm, d), BF16),
                        pltpu.VMEM((2 * fc // LANES, SUBLANES + tm, LANES), F32),
                        pltpu.VMEM((2 * fc // LANES, SUBLANES + tm, LANES), F32),
                        pltpu.VMEM((2 * fc // LANES, SUBLANES + tm, LANES), F32),
                        pltpu.VMEM((nf, 2 * fc // LANES, SUBLANES, LANES), F32),
                        pltpu.VMEM((tm, d), F32)],
        compiler_params=_params(),
        name="conv_ffn",
    )(x, mod, w_up, conv_w, conv_b, w_down, ln_g.reshape(1, d), ln_b.reshape(1, d))


def _conf_kernel(x_ref, mod_ref, w1_ref, b1_ref, dw_ref, dwb_ref, lng_ref, lnb_ref, w2_ref, b2_ref,
                 g_ref, b_ref, o_ref, cbuf, ybuf, *, tm):
    i = pl.program_id(1)
    d = D_MODEL
    halo = CONF_HALO
    ngrp = d // LANES

    @pl.when(i == 0)
    def _():
        cbuf[:, 0:halo, :] = jnp.zeros((ngrp, halo, LANES), F32)

    x = x_ref[...]
    h = _modulate(x, mod_ref, 0).astype(BF16)
    per = MXU_DIM // LANES
    first = halo - (CONV_WIDTH - 1)

    def glu_chunk(c):
        lo, hi = c * MXU_DIM, (c + 1) * MXU_DIM
        a = _dot(h, w1_ref[:, lo:hi]) + b1_ref[:, lo:hi]
        g = _dot(h, w1_ref[:, d + lo:d + hi]) + b1_ref[:, d + lo:d + hi]
        u = a * _sigmoid(g)
        for s in range(per):
            cbuf[c * per + s, halo:halo + tm, :] = u[:, s * LANES:(s + 1) * LANES]

    def conv_group(c):
        for r0 in range(0, tm, CONF_RB):
            acc = jnp.broadcast_to(dwb_ref[c], (CONF_RB, LANES))
            for k in range(CONV_WIDTH):
                acc = acc + dw_ref[c, k:k + 1, :] * cbuf[c, r0 + first + k:r0 + first + k + CONF_RB, :]
            ybuf[c, r0:r0 + CONF_RB, :] = acc
        cbuf[c, 0:halo, :] = cbuf[c, tm:tm + halo, :]

    nchunk = d // MXU_DIM
    glu_chunk(0)
    for c in range(nchunk):
        if c + 1 < nchunk:
            glu_chunk(c + 1)
        for s in range(per):
            conv_group(c * per + s)

    y = jnp.concatenate([ybuf[c] for c in range(ngrp)], axis=1)
    z = _silu(_ln_rows(y, lng_ref[...], lnb_ref[...], LN_EPS))
    out = _dot(z.astype(BF16), w2_ref[...]) + b2_ref[...]
    o_ref[...] = _post_norm(x, out, mod_ref, 0, g_ref, b_ref)


def _conf_call(x, mod, w_pw1, b_pw1, dw_w, dw_b, ln_g, ln_b, w_pw2, b_pw2, pg, pb):
    bsz, seq, d = x.shape
    tm = min(CONF_TM, seq)
    ngrp = d // LANES
    dw = dw_w.reshape(CONV_WIDTH, ngrp, LANES).transpose(1, 0, 2)
    dwb = dw_b.reshape(ngrp, 1, LANES)
    kern = functools.partial(_conf_kernel, tm=tm)
    row = lambda v: v.reshape(1, -1)
    return pl.pallas_call(
        kern,
        out_shape=jax.ShapeDtypeStruct(x.shape, F32),
        grid=(bsz, seq // tm),
        in_specs=[_tile_spec(tm), _mod_spec(),
                  _const_spec((d, 2 * d)), _const_spec((1, 2 * d)),
                  _const_spec(dw.shape), _const_spec(dwb.shape),
                  _const_spec((1, d)), _const_spec((1, d)),
                  _const_spec((d, d)), _const_spec((1, d)),
                  _const_spec((1, d)), _const_spec((1, d))],
        out_specs=_tile_spec(tm),
        scratch_shapes=[pltpu.VMEM((ngrp, CONF_HALO + tm, LANES), F32),
                        pltpu.VMEM((ngrp, tm, LANES), F32)],
        compiler_params=_params(),
        name="conformer_conv",
    )(x, mod, w_pw1.astype(BF16), row(b_pw1), dw, dwb, row(ln_g), row(ln_b),
      w_pw2.astype(BF16), row(b_pw2), row(pg), row(pb))


def _hgrn_kernel(x_ref, mod_ref, lb_ref, win_ref, ng_ref, wout_ref, tri_ref, g_ref, b_ref, o_ref,
                 st_ref, obuf, *, tm, layer):
    i = pl.program_id(1)
    d = D_MODEL
    ch = HGRN_CHUNK
    nch = tm // ch

    @pl.when(i == 0)
    def _():
        st_ref[...] = jnp.zeros_like(st_ref)

    x = x_ref[...]
    h = _modulate(x, mod_ref, 0).astype(BF16)
    f_pre = _dot(h, win_ref[:, d:2 * d])
    q_pre = _dot(h, win_ref[:, 0:d])
    v_pre = _dot(h, win_ref[:, 2 * d:3 * d])
    g_pre = _dot(h, win_ref[:, 3 * d:4 * d])
    logits = lb_ref[...]
    e = jnp.exp(logits - jnp.max(logits, axis=0, keepdims=True))
    sm = e / jnp.sum(e, axis=0, keepdims=True)
    lb = jnp.sum(sm[1:layer + 1, :], axis=0, keepdims=True) if layer > 0 else jnp.zeros((1, d), F32)
    f = lb + (1.0 - lb) * _sigmoid(f_pre)
    q = _silu(q_pre)
    k = 1.0 - f
    lf = jnp.log(f)
    tri = tri_ref[...]
    bcum = sum(_dot(tri, part) for part in _split3(lf))
    b3 = bcum.reshape(nch, ch, d)
    b_mid = b3[:, ch // 2 - 1:ch // 2, :]
    b_last = b3[:, ch - 1:ch, :]
    q3 = q.reshape(nch, ch, d)
    k3 = k.reshape(nch, ch, d)
    qs = (q3 * jnp.exp(b3 - b_mid)).reshape(tm, d).astype(BF16)
    ks = (k3 * jnp.exp(b_mid - b3)).reshape(tm, d).astype(BF16)
    qd = (q3 * jnp.exp(b3)).reshape(tm, d).astype(BF16)
    kd = (k3 * jnp.exp(b_last - b3)).reshape(tm, d).astype(BF16)
    dec = jnp.exp(b_last)
    vb = v_pre.astype(BF16)

    causal = tri.astype(F32) > 0.5

    heads = [slice(hh * HGRN_DK, (hh + 1) * HGRN_DK) for hh in range(HGRN_HEADS)]
    chunks = [slice(c * ch, (c + 1) * ch) for c in range(nch)]
    o_intra = []
    for sl in heads:
        sc = _dot_nt(qs[:, sl], ks[:, sl])
        o_intra.append(_dot(jnp.where(causal, sc, 0.0).astype(BF16), vb[:, sl]))
    kv = [[_dot_tn(vb[rows, sl], kd[rows, sl]) for sl in heads] for rows in chunks]
    st = [st_ref[hh] for hh in range(HGRN_HEADS)]
    parts = [[] for _ in heads]
    for c, rows in enumerate(chunks):
        for hh, sl in enumerate(heads):
            parts[hh].append(_dot_nt(qd[rows, sl], st[hh].astype(BF16)))
            st[hh] = dec[c, :, sl] * st[hh] + kv[c][hh]
    for hh, sl in enumerate(heads):
        st_ref[hh] = st[hh]
        o = o_intra[hh] + jnp.concatenate(parts[hh], axis=0)
        o = o * lax.rsqrt(jnp.mean(o * o, axis=-1, keepdims=True) + RMS_EPS) * ng_ref[...]
        obuf[:, sl] = o
    gate = _silu(g_pre)
    y = _dot((obuf[...] * gate).astype(BF16), wout_ref[...])
    o_ref[...] = _post_norm(x, y, mod_ref, 0, g_ref, b_ref)


def _hgrn_call(x, mod, layer, lb_logits, w_in, norm_g, w_out, pg, pb):
    bsz, seq, d = x.shape
    tm = min(HGRN_TM, seq)
    r = jnp.arange(tm)
    tri = ((r[:, None] // HGRN_CHUNK == r[None, :] // HGRN_CHUNK) & (r[None, :] <= r[:, None])).astype(BF16)
    kern = functools.partial(_hgrn_kernel, tm=tm, layer=layer)
    row = lambda v: v.reshape(1, -1)
    return pl.pallas_call(
        kern,
        out_shape=jax.ShapeDtypeStruct(x.shape, F32),
        grid=(bsz, seq // tm),
        in_specs=[_tile_spec(tm), _mod_spec(), _const_spec(lb_logits.shape),
                  _const_spec((d, 4 * d)), _const_spec((1, HGRN_DK)), _const_spec((d, d)),
                  _const_spec((tm, tm)), _const_spec((1, d)), _const_spec((1, d))],
        out_specs=_tile_spec(tm),
        scratch_shapes=[pltpu.VMEM((HGRN_HEADS, HGRN_DK, HGRN_DK), F32),
                        pltpu.VMEM((tm, d), F32)],
        compiler_params=_params(),
        name="hgrn2",
    )(x, mod, lb_logits.astype(F32), w_in.astype(BF16), row(norm_g), w_out.astype(BF16), tri, row(pg), row(pb))


def _mlstm_kernel(x_ref, mod_ref, wup_ref, cw_ref, cb_ref, wqk_ref, wv_ref, wg_ref, bg_ref,
                  ng_ref, sk_ref, wdn_ref, tri_ref, g_ref, b_ref, o_ref,
                  xbuf, qbuf, kbuf, vbuf, c_ref, n_ref, m_ref, hbuf, *, tm):
    i = pl.program_id(1)
    d = D_MODEL
    inner = MLSTM_INNER
    hd = MLSTM_HDIM
    nh = MLSTM_HEADS
    blk = MXU_DIM
    pad = SUBLANES

    @pl.when(i == 0)
    def _():
        c_ref[...] = jnp.zeros_like(c_ref)
        n_ref[...] = jnp.zeros_like(n_ref)
        m_ref[...] = jnp.zeros_like(m_ref)
        xbuf[:, 0:pad, :] = jnp.zeros((inner // LANES, pad, LANES), F32)

    x = x_ref[...]
    h = _modulate(x, mod_ref, 0).astype(BF16)
    xm = _dot(h, wup_ref[:, 0:inner])
    convs = []
    for g in range(inner // LANES):
        lanes = slice(g * LANES, (g + 1) * LANES)
        xbuf[g, pad:pad + tm, :] = xm[:, lanes]
        cv = cw_ref[MLSTM_CONV - 1:MLSTM_CONV, lanes] * xm[:, lanes] + cb_ref[:, lanes]
        for kk in range(MLSTM_CONV - 1):
            off = pad - (MLSTM_CONV - 1) + kk
            cv = cv + cw_ref[kk:kk + 1, lanes] * xbuf[g, off:off + tm, :]
        convs.append(cv)
        xbuf[g, 0:pad, :] = xm[tm - pad:tm, lanes]
    xc = _silu(jnp.concatenate(convs, axis=1))
    xcb = xc.astype(BF16)
    xmb = xm.astype(BF16)
    for j in range(inner // blk):
        cols = slice(j * blk, (j + 1) * blk)
        qk = _dot(xcb[:, cols], wqk_ref[j])
        qbuf[:, cols] = qk[:, 0:blk]
        kbuf[:, cols] = qk[:, blk:2 * blk]
        vbuf[:, cols] = _dot(xmb[:, cols], wv_ref[j])
    qf = qbuf[...]
    kf = kbuf[...]
    vf = vbuf[...]
    qb = qf.astype(BF16)
    vb = vf.astype(BF16)
    gates = (_dot(qb, wg_ref[0:inner, :]) + _dot(kf.astype(BF16), wg_ref[inner:2 * inner, :])
             + _dot(vb, wg_ref[2 * inner:3 * inner, :]) + bg_ref[...])
    log_f = jnp.minimum(gates, 0.0) - jnp.log(1.0 + jnp.exp(-jnp.abs(gates)))
    tri = tri_ref[...]
    bc = sum(_dot(tri, part) for part in _split3(log_f))
    gates_t = gates.T
    bc_t = bc.T
    row = lax.broadcasted_iota(jnp.int32, (tm, tm), 0)
    col = lax.broadcasted_iota(jnp.int32, (tm, tm), 1)
    causal = col <= row
    kscaled = kf * (hd ** -0.5)

    heads = [slice(hh * hd, (hh + 1) * hd) for hh in range(nh)]
    kh = [kscaled[:, sl] for sl in heads]
    cm = [c_ref[hh] for hh in range(nh)]
    qk = [_dot_nt(qb[:, sl], kh[hh].astype(BF16)) for hh, sl in enumerate(heads)]
    qc = [_dot(qb[:, sl], cm[hh].astype(BF16)) for hh, sl in enumerate(heads)]
    zg = [_dot(h, wup_ref[:, inner + sl.start:inner + sl.stop]) for sl in heads]
    for hh, sl in enumerate(heads):
        bt_col = bc[:, nh + hh:nh + hh + 1]
        it_col = gates[:, hh:hh + 1]
        bt_row = bc_t[nh + hh:nh + hh + 1, :]
        it_row = gates_t[hh:hh + 1, :]
        m_prev = m_ref[hh][0:1, 0:1]
        log_w = jnp.where(causal, bt_col - bt_row + it_row, -jnp.inf)
        log_inter = bt_col + m_prev
        m_t = jnp.maximum(jnp.max(log_w, axis=-1, keepdims=True), log_inter)
        vhb = vb[:, sl]
        s_qk = qk[hh] * jnp.exp(log_w - m_t)
        w_inter = jnp.exp(log_inter - m_t)
        nvec = n_ref[hh]
        b_last = bt_col[tm - 1:tm, :]
        log_ws = b_last - bt_col + it_col
        m_new = jnp.maximum(b_last + m_prev, jnp.max(log_ws, axis=0, keepdims=True))
        ws = jnp.exp(log_ws - m_new)
        decay = jnp.exp(b_last + m_prev - m_new)
        kw = kh[hh] * ws
        c_ref[hh] = decay * cm[hh] + _dot_tn(kw.astype(BF16), vhb)
        n_ref[hh] = decay * nvec + jnp.sum(kw, axis=0, keepdims=True)
        m_ref[hh] = jnp.broadcast_to(m_new, (SUBLANES, LANES))
        num = _dot(s_qk.astype(BF16), vhb) + w_inter * qc[hh]
        den = (jnp.sum(s_qk, axis=-1, keepdims=True)
               + w_inter * jnp.sum(qf[:, sl] * nvec, axis=-1, keepdims=True))
        h_out = num / jnp.maximum(jnp.abs(den), jnp.exp(-m_t))
        mu = jnp.mean(h_out, axis=-1, keepdims=True)
        dlt = h_out - mu
        var = jnp.mean(dlt * dlt, axis=-1, keepdims=True)
        hn = dlt * lax.rsqrt(var + LN_EPS)
        hbuf[:, sl] = ((hn * ng_ref[:, sl] + sk_ref[:, sl] * xc[:, sl]) * _silu(zg[hh])).astype(BF16)
    y = _dot(hbuf[...], wdn_ref[...])
    o_ref[...] = _post_norm(x, y, mod_ref, 0, g_ref, b_ref)


def _block_diag_tiles(w, tile):
    g, blk, _ = w.shape
    per = tile // blk
    rows = w.reshape(g // per, tile, blk)
    r = jnp.arange(tile)
    same_block = (r[:, None] // blk == r[None, :] // blk).astype(w.dtype)
    return jnp.tile(rows, (1, 1, per)) * same_block


def _mlstm_call(x, mod, w_up, conv_w, conv_b, w_q, w_k, w_v, w_gates, b_gates, norm_g, skip, w_down, pg, pb):
    bsz, seq, d = x.shape
    tm = min(MLSTM_TM, seq)
    inner, nh, hd = MLSTM_INNER, MLSTM_HEADS, MLSTM_HDIM
    wq_t = _block_diag_tiles(w_q, MXU_DIM)
    wk_t = _block_diag_tiles(w_k, MXU_DIM)
    wqk = jnp.concatenate([wq_t, wk_t], axis=-1).astype(BF16)
    wv = _block_diag_tiles(w_v, MXU_DIM).astype(BF16)
    wg = jnp.pad(w_gates, ((0, 0), (0, LANES - 2 * nh))).astype(BF16)
    bg = jnp.pad(b_gates, (0, LANES - 2 * nh)).reshape(1, LANES)
    r = jnp.arange(tm)
    tri = (r[None, :] <= r[:, None]).astype(BF16)
    kern = functools.partial(_mlstm_kernel, tm=tm)
    row = lambda v: v.reshape(1, -1)
    return pl.pallas_call(
        kern,
        out_shape=jax.ShapeDtypeStruct(x.shape, F32),
        grid=(bsz, seq // tm),
        in_specs=[_tile_spec(tm), _mod_spec(),
                  _const_spec((d, 2 * inner)), _const_spec((MLSTM_CONV, inner)), _const_spec((1, inner)),
                  _const_spec(wqk.shape), _const_spec(wv.shape),
                  _const_spec(wg.shape), _const_spec((1, LANES)),
                  _const_spec((1, inner)), _const_spec((1, inner)), _const_spec((inner, d)),
                  _const_spec((tm, tm)), _const_spec((1, d)), _const_spec((1, d))],
        out_specs=_tile_spec(tm),
        scratch_shapes=[pltpu.VMEM((inner // LANES, SUBLANES + tm, LANES), F32),
                        pltpu.VMEM((tm, inner), F32), pltpu.VMEM((tm, inner), F32),
                        pltpu.VMEM((tm, inner), F32),
                        pltpu.VMEM((nh, hd, hd), F32), pltpu.VMEM((nh, 1, hd), F32),
                        pltpu.VMEM((nh, SUBLANES, LANES), F32),
                        pltpu.VMEM((tm, inner), BF16)],
        compiler_params=_params(),
        name="mlstm",
    )(x, mod, w_up.astype(BF16), conv_w, row(conv_b), wqk, wv, wg, bg,
      row(norm_g), row(skip), w_down.astype(BF16), tri, row(pg), row(pb))


def _sb_kernel(x_ref, mod_ref, wqkv_ref, wout_ref, g_ref, b_ref, o_ref,
               q_scr, k_scr, v_scr, z_all, w_all, r_all, pav, later_scr, acc_scr, bias_scr, cum_scr,
               *, t):
    i = pl.program_id(1)
    d = D_MODEL
    nhp = d // LANES
    nblk = i + 1
    x = x_ref[...]
    h = _modulate(x, mod_ref, 0).astype(BF16)
    lane = lax.broadcasted_iota(jnp.int32, (1, LANES), 1)
    head_masks = (lane < SB_HDIM, lane >= SB_HDIM)
    base = pl.multiple_of(i * t, t)
    q = _dot(h, wqkv_ref[:, 0:d]) * (SB_HDIM ** -0.5 * LOG2_E)
    kk = _dot(h, wqkv_ref[:, d:2 * d])
    vv = _dot(h, wqkv_ref[:, 2 * d:3 * d])
    for hp in range(nhp):
        sl = slice(hp * LANES, (hp + 1) * LANES)
        k_scr[hp, pl.ds(base, t), :] = kk[:, sl].astype(BF16)
        for s in range(2):
            q_scr[hp, s * t:(s + 1) * t, :] = jnp.where(head_masks[s], q[:, sl], 0.0).astype(BF16)
            v_scr[hp, s, pl.ds(base, t), :] = jnp.where(head_masks[s], vv[:, sl], 0.0).astype(BF16)

    row = lax.broadcasted_iota(jnp.int32, (t, t), 0)
    col = lax.broadcasted_iota(jnp.int32, (t, t), 1)
    bias_scr[0] = jnp.zeros((t, t), F32)
    bias_scr[1] = jnp.where(col < row, 0.0, SB_MASKED)
    incl_ones = jnp.where(row >= col, 1.0, 0.0).astype(BF16)
    cum_scr[0:t, :] = incl_ones
    cum_scr[t:2 * t, :] = incl_ones
    pav[...] = jnp.zeros_like(pav)
    later_scr[...] = jnp.zeros_like(later_scr)
    acc_scr[...] = jnp.zeros_like(acc_scr)

    def scores(item, zbuf):
        hp, jj = item
        start = pl.multiple_of((i - jj) * t, t)
        z = _dot_nt(q_scr[hp], k_scr[hp, pl.ds(start, t), :])
        zbuf[...] = (z.reshape(2, t, t) + bias_scr[jnp.where(jj == 0, 1, 0)]).reshape(2 * t, t)

    def cumulate(zbuf, wbuf, rbuf):
        z = zbuf[...]
        sp = jnp.maximum(z, 0.0) + jnp.log(1.0 + jnp.exp2(-jnp.abs(z))) * LOG2_E
        hi = sp.astype(BF16)
        lo = (sp - hi.astype(F32)).astype(BF16)
        incl = _dot(jnp.concatenate([hi, lo], axis=1), cum_scr[...])
        wbuf[...] = z - incl
        rbuf[...] = jnp.broadcast_to(incl[:, 0:1], (2 * t, LANES))

    def weights(item, wbuf, rbuf):
        hp, jj = item
        later = jnp.where(jj == 0, 0.0, later_scr[...])
        a = jnp.exp2(wbuf[...] - jnp.concatenate([later] * (t // LANES), axis=1)).astype(BF16)
        later_scr[...] = later + rbuf[...]
        start = pl.multiple_of((i - jj) * t, t)
        vst = jnp.concatenate([v_scr[hp, 0, pl.ds(start, t), :], v_scr[hp, 1, pl.ds(start, t), :]], axis=0)
        return _dot(jnp.concatenate([a[0:t], a[t:2 * t]], axis=1), vst)

    def advance(item):
        hp, jj = item
        last = jj + 1 >= nblk
        return (jnp.where(last, jnp.minimum(hp + 1, nhp - 1), hp), jnp.where(last, 0, jj + 1))

    def step(hp_prev, items, src, dst, with_scores=True, with_cumulate=True):
        it0, it1, _, _, it4, it5 = items
        if with_scores:
            scores(it4, z_all.at[dst])
            scores(it5, z_all.at[dst + 1])
        if with_cumulate:
            cumulate(z_all.at[src], w_all.at[dst], r_all.at[dst])
            cumulate(z_all.at[src + 1], w_all.at[dst + 1], r_all.at[dst + 1])
        acc_scr[hp_prev] += pav[...]
        p0 = weights(it0, w_all.at[src], r_all.at[src])
        pav[...] = weights(it1, w_all.at[src + 1], r_all.at[src + 1])
        acc_scr[it0[0]] += p0

    def six(it0):
        items = [it0]
        for _ in range(5):
            items.append(advance(items[-1]))
        return items

    first = six((jnp.int32(0), jnp.int32(0)))
    scores(first[0], z_all.at[2])
    scores(first[1], z_all.at[3])
    cumulate(z_all.at[2], w_all.at[0], r_all.at[0])
    cumulate(z_all.at[3], w_all.at[1], r_all.at[1])
    scores(first[2], z_all.at[0])
    scores(first[3], z_all.at[1])

    def body(_, carry):
        items = six(carry[1:])
        step(carry[0], items, 0, 2)
        items2 = six(items[2])
        step(items[1][0], items2, 2, 0)
        return (items2[1][0],) + items2[2]

    trips = (nhp // 2) * nblk
    carry = lax.fori_loop(0, trips // 2 - 1, body, (jnp.int32(0),) + first[0])
    items = six(carry[1:])
    step(carry[0], items, 0, 2, with_scores=False)
    items2 = six(items[2])
    step(items[1][0], items2, 2, 0, with_scores=False, with_cumulate=False)
    acc_scr[items2[1][0]] += pav[...]
    o = jnp.concatenate([acc_scr[hp] for hp in range(nhp)], axis=1).astype(BF16)
    y = _dot(o, wout_ref[...])
    o_ref[...] = _post_norm(x, y, mod_ref, 0, g_ref, b_ref)


def _sb_call(x, mod, w_qkv, w_out, pg, pb):
    bsz, seq, d = x.shape
    t = min(SB_T, seq)
    nhp = d // LANES
    kern = functools.partial(_sb_kernel, t=t)
    row = lambda v: v.reshape(1, -1)
    return pl.pallas_call(
        kern,
        out_shape=jax.ShapeDtypeStruct(x.shape, F32),
        grid=(bsz, seq // t),
        in_specs=[_tile_spec(t), _mod_spec(), _const_spec((d, 3 * d)), _const_spec((d, d)),
                  _const_spec((1, d)), _const_spec((1, d))],
        out_specs=_tile_spec(t),
        scratch_shapes=[pltpu.VMEM((nhp, 2 * t, LANES), BF16),
                        pltpu.VMEM((nhp, seq, LANES), BF16),
                        pltpu.VMEM((nhp, 2, seq, LANES), BF16),
                        pltpu.VMEM((4, 2 * t, t), F32), pltpu.VMEM((4, 2 * t, t), F32),
                        pltpu.VMEM((4, 2 * t, LANES), F32),
                        pltpu.VMEM((t, LANES), F32),
                        pltpu.VMEM((2 * t, LANES), F32),
                        pltpu.VMEM((nhp, t, LANES), F32),
                        pltpu.VMEM((2, t, t), F32),
                        pltpu.VMEM((2 * t, t), BF16)],
        compiler_params=_params(),
        name="stick_breaking",
    )(x, mod, w_qkv.astype(BF16), w_out.astype(BF16), row(pg), row(pb))


def kernel(x, c, ada_w, ada_b, post_ln_g, post_ln_b, ffn_w_up, ffn_conv_w, ffn_conv_b, ffn_w_down, cc_w_pw1, cc_b_pw1, cc_dw_w, cc_dw_b, cc_ln_g, cc_ln_b, cc_w_pw2, cc_b_pw2, hg_lb_logits, hg_w_in, hg_norm_g, hg_w_out, ml_w_up, ml_conv_w, ml_conv_b, ml_w_q, ml_w_k, ml_w_v, ml_w_gates, ml_b_gates, ml_norm_g, ml_skip, ml_w_down, sb_w_qkv, sb_w_out):
    bsz = x.shape[0]
    mods = _ada_call(c, ada_w, ada_b).reshape(DEPTH, bsz, 6, D_MODEL)
    ffn_wup_b = ffn_w_up.astype(BF16)
    ffn_wdn_b = ffn_w_down.astype(BF16)
    ffn_cb = ffn_conv_b.reshape(DEPTH, 1, -1)
    for i in range(DEPTH):
        kind, j = i % 4, i // 4
        mod = mods[i]
        pg, pb = post_ln_g[i, 0], post_ln_b[i, 0]
        if kind == 0:
            x = _conf_call(x, mod, cc_w_pw1[j], cc_b_pw1[j], cc_dw_w[j], cc_dw_b[j], cc_ln_g[j], cc_ln_b[j],
                           cc_w_pw2[j], cc_b_pw2[j], pg, pb)
        elif kind == 1:
            x = _hgrn_call(x, mod, i, hg_lb_logits, hg_w_in[j], hg_norm_g[j], hg_w_out[j], pg, pb)
        elif kind == 2:
            x = _mlstm_call(x, mod, ml_w_up[j], ml_conv_w[j], ml_conv_b[j], ml_w_q[j], ml_w_k[j], ml_w_v[j],
                            ml_w_gates[j], ml_b_gates[j], ml_norm_g[j], ml_skip[j], ml_w_down[j], pg, pb)
        else:
            x = _sb_call(x, mod, sb_w_qkv[j], sb_w_out[j], pg, pb)
        x = _ffn_call(x, mod, i, ffn_wup_b, ffn_conv_w, ffn_cb, ffn_wdn_b, post_ln_g[i, 1], post_ln_b[i, 1])
    return x
```
